```python
import math
import jax
import jax.numpy as jnp
from jax import lax
import numpy as np

D_MODEL = 1024
BATCH = 8
SEQ = 8192
DEPTH = 1

MEM_LEN = 256
LRU_WIDTH = D_MODEL // 2
LRU_HEADS = 8
LRU_HEAD_DIM = LRU_WIDTH // LRU_HEADS
CONV_WIDTH = 4
LRU_C = 8.0
SB_WIDTH = D_MODEL - LRU_WIDTH
SB_HEADS = 8
SB_HEAD_DIM = SB_WIDTH // SB_HEADS
MIX_WIDTH = LRU_WIDTH + SB_WIDTH
IN_WIDTH = 2 * LRU_WIDTH + 3 * SB_WIDTH
Q_BLOCK = 128
D_FF = 128 * round(8 * D_MODEL / 3 / 128)
MEM_HEADS = 4
MEM_HEAD_DIM = D_MODEL // MEM_HEADS
ALPHA = (2 * DEPTH) ** 0.25
BETA = (8 * DEPTH) ** -0.25
LN_EPS = 1e-5
RMS_EPS = 1e-6

kernel_name = "hybrid_rglru_stickbreaking_deepnorm_layer"


def _layer_norm(x, g, b):
    xf = x.astype(jnp.float32)
    mu = jnp.mean(xf, axis=-1, keepdims=True)
    var = jnp.mean(jnp.square(xf - mu), axis=-1, keepdims=True)
    return ((xf - mu) * lax.rsqrt(var + LN_EPS) * g + b).astype(x.dtype)


def _rms_norm(x, g):
    xf = x.astype(jnp.float32)
    return xf * lax.rsqrt(jnp.mean(jnp.square(xf), axis=-1, keepdims=True) + RMS_EPS) * g


def _swiglu(h, w13, w2):
    gate, up = jnp.split(h @ w13, 2, axis=-1)
    return (jax.nn.silu(gate) * up) @ w2


def _causal_depthwise_conv(x, w, b):
    s = x.shape[1]
    xp = jnp.pad(x, ((0, 0), (CONV_WIDTH - 1, 0), (0, 0)))
    out = b
    for k in range(CONV_WIDTH):
        out = out + xp[:, k:k + s] * w[k]
    return out


def _lru_combine(left, right):
    a_l, b_l = left
    a_r, b_r = right
    return a_l * a_r, a_r * b_l + b_r


def _rglru(xb, conv_w, conv_b, w_rg, b_rg, w_ig, b_ig, lam):
    bsz, s, _ = xb.shape
    xc = _causal_depthwise_conv(xb.astype(jnp.float32), conv_w, conv_b)
    xh = xc.reshape(bsz, s, LRU_HEADS, LRU_HEAD_DIM)
    r = jax.nn.sigmoid(jnp.einsum('bshi,hij->bshj', xh, w_rg).reshape(bsz, s, LRU_WIDTH) + b_rg)
    i = jax.nn.sigmoid(jnp.einsum('bshi,hij->bshj', xh, w_ig).reshape(bsz, s, LRU_WIDTH) + b_ig)
    log_a = -LRU_C * r * jax.nn.softplus(-lam)
    a = jnp.exp(log_a)
    u = jnp.sqrt(-jnp.expm1(2.0 * log_a)) * (i * xc)
    _, h = lax.associative_scan(_lru_combine, (a, u), axis=1)
    return h


def _stick_breaking_block(qb, kp, vp, q0):
    z = jnp.einsum('bhqd,bhkd->bhqk', qb, kp) / math.sqrt(SB_HEAD_DIM)
    t_idx = q0 + jnp.arange(qb.shape[2])[:, None]
    s_idx = jnp.arange(kp.shape[2])[None, :]
    causal = s_idx < t_idx
    log_not = jnp.where(causal, jax.nn.log_sigmoid(-z), 0.0)
    suffix = lax.cumsum(log_not, axis=3, reverse=True) - log_not
    weight = jnp.where(causal, jnp.exp(jax.nn.log_sigmoid(z) + suffix), 0.0)
    return jnp.einsum('bhqk,bhkd->bhqd', weight, vp)


def _stick_breaking_attention(q, k, v):
    bsz, s, _ = q.shape
    def heads(t):
        return t.astype(jnp.float32).reshape(bsz, s, SB_HEADS, SB_HEAD_DIM).transpose(0, 2, 1, 3)
    q, k, v = heads(q), heads(k), heads(v)
    outs = []
    for blk in range(s // Q_BLOCK):
        q0 = blk * Q_BLOCK
        end = q0 + Q_BLOCK
        outs.append(_stick_breaking_block(q[:, :, q0:end], k[:, :, :end], v[:, :, :end], q0))
    o = jnp.concatenate(outs, axis=2)
    return o.transpose(0, 2, 1, 3).reshape(bsz, s, SB_WIDTH)


def _hybrid_mixer(h, w_in, conv_w, conv_b, w_rg, b_rg, w_ig, b_ig, lam, g_lru, g_sb, w_out):
    proj = h @ w_in
    splits = [LRU_WIDTH, 2 * LRU_WIDTH, 2 * LRU_WIDTH + SB_WIDTH, 2 * LRU_WIDTH + 2 * SB_WIDTH]
    xb, gate, q, k, v = jnp.split(proj, splits, axis=-1)
    y_lru = _rglru(xb, conv_w, conv_b, w_rg, b_rg, w_ig, b_ig, lam) * jax.nn.gelu(gate.astype(jnp.float32))
    y_sb = _stick_breaking_attention(q, k, v)
    y = jnp.concatenate([_rms_norm(y_lru, g_lru), _rms_norm(y_sb, g_sb)], axis=-1)
    return y @ w_out


def _memory_cross_attention(h, mem, wq, wkv, wo):
    bsz, s, _ = h.shape
    q = (h @ wq).reshape(bsz, s, MEM_HEADS, MEM_HEAD_DIM).astype(jnp.float32)
    km, vm = jnp.split(mem @ wkv, 2, axis=-1)
    km = km.reshape(bsz, -1, MEM_HEADS, MEM_HEAD_DIM).astype(jnp.float32)
    vm = vm.reshape(bsz, -1, MEM_HEADS, MEM_HEAD_DIM).astype(jnp.float32)
    scores = jnp.einsum('bshd,bmhd->bhsm', q, km) / math.sqrt(MEM_HEAD_DIM)
    p = jax.nn.softmax(scores, axis=-1)
    o = jnp.einsum('bhsm,bmhd->bshd', p, vm).reshape(bsz, s, D_MODEL)
    return o @ wo


def setup_inputs(seed: int = 0) -> dict:
    key = jax.random.key(seed)
    ks = jax.random.split(key, 32)
    f32 = jnp.float32
    L = DEPTH

    def nrm(k, shape, scale):
        return jax.random.normal(k, shape, f32) * scale

    x = nrm(ks[0], (BATCH, SEQ, D_MODEL), 1.0)
    mem = nrm(ks[1], (BATCH, MEM_LEN, D_MODEL), 1.0)
    ffn1_w13 = nrm(ks[2], (L, D_MODEL, 2 * D_FF), BETA * D_MODEL ** -0.5)
    ffn1_w2 = nrm(ks[3], (L, D_FF, D_MODEL), BETA * D_FF ** -0.5)
    ln1_g = 1.0 + nrm(ks[4], (L, D_MODEL), 0.02)
    ln1_b = nrm(ks[5], (L, D_MODEL), 0.02)
    in_scale = jnp.concatenate([jnp.ones((IN_WIDTH - SB_WIDTH,), f32), jnp.full((SB_WIDTH,), BETA, f32)])
    w_in = nrm(ks[6], (L, D_MODEL, IN_WIDTH), D_MODEL ** -0.5) * in_scale
    conv_w = nrm(ks[7], (L, CONV_WIDTH, LRU_WIDTH), CONV_WIDTH ** -0.5)
    conv_b = nrm(ks[8], (L, LRU_WIDTH), 0.01)
    w_rgate = nrm(ks[9], (L, LRU_HEADS, LRU_HEAD_DIM, LRU_HEAD_DIM), LRU_HEAD_DIM ** -0.5)
    b_rgate = nrm(ks[10], (L, LRU_WIDTH), 0.01)
    w_igate = nrm(ks[11], (L, LRU_HEADS, LRU_HEAD_DIM, LRU_HEAD_DIM), LRU_HEAD_DIM ** -0.5)
    b_igate = nrm(ks[12], (L, LRU_WIDTH), 0.01)
    a_pow_c = jax.random.uniform(ks[13], (L, LRU_WIDTH), f32, 0.9, 0.999)
    log_a0 = jnp.log(a_pow_c) / LRU_C
    lru_lambda = log_a0 - jnp.log(-jnp.expm1(log_a0))
    g_lru = 1.0 + nrm(ks[14], (L, LRU_WIDTH), 0.02)
    g_sb = 1.0 + nrm(ks[15], (L, SB_WIDTH), 0.02)
    w_out = nrm(ks[16], (L, MIX_WIDTH, D_MODEL), BETA * MIX_WIDTH ** -0.5)
    ln2_g = 1.0 + nrm(ks[17], (L, D_MODEL), 0.02)
    ln2_b = nrm(ks[18], (L, D_MODEL), 0.02)
    mem_wq = nrm(ks[19], (L, D_MODEL, D_MODEL), D_MODEL ** -0.5)
    kv_scale = jnp.concatenate([jnp.ones((D_MODEL,), f32), jnp.full((D_MODEL,), BETA, f32)])
    mem_wkv = nrm(ks[20], (L, D_MODEL, 2 * D_MODEL), D_MODEL ** -0.5) * kv_scale
    mem_wo = nrm(ks[21], (L, D_MODEL, D_MODEL), BETA * D_MODEL ** -0.5)
    ln3_g = 1.0 + nrm(ks[22], (L, D_MODEL), 0.02)
    ln3_b = nrm(ks[23], (L, D_MODEL), 0.02)
    ffn2_w13 = nrm(ks[24], (L, D_MODEL, 2 * D_FF), BETA * D_MODEL ** -0.5)
    ffn2_w2 = nrm(ks[25], (L, D_FF, D_MODEL), BETA * D_FF ** -0.5)
    ln4_g = 1.0 + nrm(ks[26], (L, D_MODEL), 0.02)
    ln4_b = nrm(ks[27], (L, D_MODEL), 0.02)
    return {
        "x": x, "mem": mem,
        "ffn1_w13": ffn1_w13, "ffn1_w2": ffn1_w2, "ln1_g": ln1_g, "ln1_b": ln1_b,
        "w_in": w_in, "conv_w": conv_w, "conv_b": conv_b,
        "w_rgate": w_rgate, "b_rgate": b_rgate, "w_igate": w_igate, "b_igate": b_igate,
        "lru_lambda": lru_lambda, "g_lru": g_lru, "g_sb": g_sb, "w_out": w_out,
        "ln2_g": ln2_g, "ln2_b": ln2_b,
        "mem_wq": mem_wq, "mem_wkv": mem_wkv, "mem_wo": mem_wo, "ln3_g": ln3_g, "ln3_b": ln3_b,
        "ffn2_w13": ffn2_w13, "ffn2_w2": ffn2_w2, "ln4_g": ln4_g, "ln4_b": ln4_b,
    }


def reference(x, mem, ffn1_w13, ffn1_w2, ln1_g, ln1_b, w_in, conv_w, conv_b,
              w_rgate, b_rgate, w_igate, b_igate, lru_lambda, g_lru, g_sb, w_out,
              ln2_g, ln2_b, mem_wq, mem_wkv, mem_wo, ln3_g, ln3_b,
              ffn2_w13, ffn2_w2, ln4_g, ln4_b):
    h = x
    for l in range(DEPTH):
        h = _layer_norm(ALPHA * h + 0.5 * _swiglu(h, ffn1_w13[l], ffn1_w2[l]), ln1_g[l], ln1_b[l])
        mix = _hybrid_mixer(h, w_in[l], conv_w[l], conv_b[l], w_rgate[l], b_rgate[l],
                            w_igate[l], b_igate[l], lru_lambda[l], g_lru[l], g_sb[l], w_out[l])
        h = _layer_norm(ALPHA * h + mix, ln2_g[l], ln2_b[l])
        cross = _memory_cross_attention(h, mem, mem_wq[l], mem_wkv[l], mem_wo[l])
        h = _layer_norm(ALPHA * h + cross, ln3_g[l], ln3_b[l])
        h = _layer_norm(ALPHA * h + 0.5 * _swiglu(h, ffn2_w13[l], ffn2_w2[l]), ln4_g[l], ln4_b[l])
    return h.astype(x.dtype)
```

```python
import functools
import math

import jax
import jax.numpy as jnp
from jax import lax
from jax.experimental import pallas as pl
from jax.experimental.pallas import tpu as pltpu

F32 = jnp.float32
BF16 = jnp.bfloat16

D_MODEL = 1024
DEPTH = 1
LRU_WIDTH = D_MODEL // 2
LRU_HEADS = 8
LRU_HEAD_DIM = LRU_WIDTH // LRU_HEADS
CONV_WIDTH = 4
LRU_C = 8.0
SB_WIDTH = D_MODEL - LRU_WIDTH
SB_HEADS = 8
SB_HEAD_DIM = SB_WIDTH // SB_HEADS
D_FF = 2688
MEM_HEADS = 4
MEM_HEAD_DIM = D_MODEL // MEM_HEADS
ALPHA = (2 * DEPTH) ** 0.25
LN_EPS = 1e-5
RMS_EPS = 1e-6

LOG2E = 1.4426950408889634

VMEM_LIMIT_BYTES = 56 * 1024 * 1024

TOKEN_TILE = 512
FF_CHUNKS = (1280, 1408)
SB_BLOCK = 256
LRU_CHUNK = 512


def _cparams(*sem):
    return pltpu.CompilerParams(dimension_semantics=sem, vmem_limit_bytes=VMEM_LIMIT_BYTES)


def _const_spec(shape):
    nd = len(shape)
    return pl.BlockSpec(shape, lambda *_: (0,) * nd, pipeline_mode=pl.Buffered(1))


def _layer_norm(r, g, b):
    mu = jnp.mean(r, axis=-1, keepdims=True)
    d = r - mu
    var = jnp.mean(d * d, axis=-1, keepdims=True)
    return d * lax.rsqrt(var + LN_EPS) * g + b


def _sigmoid(x):
    return 1.0 / (1.0 + jnp.exp(-x))


def _ffn_ln_kernel(x_ref, w13_ref, w2_ref, g_ref, b_ref, o_ref):
    x = x_ref[...]
    xb = x.astype(BF16)
    y = None
    off = 0
    for width in FF_CHUNKS:
        gate = jnp.dot(xb, w13_ref[:, off:off + width], preferred_element_type=F32)
        up = jnp.dot(xb, w13_ref[:, D_FF + off:D_FF + off + width], preferred_element_type=F32)
        act = (gate * _sigmoid(gate) * up).astype(BF16)
        part = jnp.dot(act, w2_ref[off:off + width, :], preferred_element_type=F32)
        y = part if y is None else y + part
        off += width
    o_ref[...] = _layer_norm(ALPHA * x + 0.5 * y, g_ref[...], b_ref[...])


def _ffn_ln(x, w13, w2, g, b):
    n = x.shape[0]
    tm = TOKEN_TILE
    return pl.pallas_call(
        _ffn_ln_kernel,
        grid=(n // tm,),
        in_specs=[
            pl.BlockSpec((tm, D_MODEL), lambda i: (i, 0)),
            _const_spec((D_MODEL, 2 * D_FF)),
            _const_spec((D_FF, D_MODEL)),
            _const_spec((1, D_MODEL)),
            _const_spec((1, D_MODEL)),
        ],
        out_specs=pl.BlockSpec((tm, D_MODEL), lambda i: (i, 0)),
        out_shape=jax.ShapeDtypeStruct((n, D_MODEL), F32),
        compiler_params=_cparams("parallel"),
        name="ffn_ln",
    )(x, w13, w2, g, b)


def _mixer_in_kernel(h_ref, wxg_ref, wqkv_t_ref, xg_ref, qkv_ref):
    hb = h_ref[...].astype(BF16)
    xg_ref[...] = jnp.dot(hb, wxg_ref[...], preferred_element_type=F32)
    t = lax.dot_general(wqkv_t_ref[...], hb, (((1,), (1,)), ((), ())), preferred_element_type=F32)
    tm = hb.shape[0]
    nh = 3 * SB_HEADS
    for c in range(tm // SB_BLOCK):
        blk = t[:, c * SB_BLOCK:(c + 1) * SB_BLOCK]
        for j in range(nh):
            rows = blk[j * SB_HEAD_DIM:(j + 1) * SB_HEAD_DIM, :]
            if j < SB_HEADS:
                rows = rows * (LOG2E / math.sqrt(SB_HEAD_DIM))
            qkv_ref[0, j, c] = rows.astype(BF16)


def _mixer_in(h, wxg, wqkv_t, bsz, seq):
    n = h.shape[0]
    tm = TOKEN_TILE
    per_b = seq // tm
    nbt = tm // SB_BLOCK
    nh = 3 * SB_HEADS
    return pl.pallas_call(
        _mixer_in_kernel,
        grid=(n // tm,),
        in_specs=[
            pl.BlockSpec((tm, D_MODEL), lambda i: (i, 0)),
            _const_spec((D_MODEL, 2 * LRU_WIDTH)),
            _const_spec((3 * SB_WIDTH, D_MODEL)),
        ],
        out_specs=[
            pl.BlockSpec((tm, 2 * LRU_WIDTH), lambda i: (i, 0)),
            pl.BlockSpec((1, nh, nbt, SB_HEAD_DIM, SB_BLOCK),
                         lambda i: (i // per_b, 0, i % per_b, 0, 0)),
        ],
        out_shape=[
            jax.ShapeDtypeStruct((n, 2 * LRU_WIDTH), F32),
            jax.ShapeDtypeStruct((bsz, nh, seq // SB_BLOCK, SB_HEAD_DIM, SB_BLOCK), BF16),
        ],
        compiler_params=_cparams("parallel"),
        name="mixer_in",
    )(h, wxg, wqkv_t)


def _rglru_kernel(xg_ref, cw_ref, cb_ref, wr_ref, br_ref, wi_ref, bi_ref, lam_ref, g_ref,
                  o_ref, xpad_ref, a_ref, u_ref, hs_ref, hcar_ref):
    ts = LRU_CHUNK
    c = pl.program_id(1)

    @pl.when(c == 0)
    def _():
        xpad_ref[0:8, :] = jnp.zeros((8, LRU_WIDTH), F32)
        hcar_ref[...] = jnp.zeros_like(hcar_ref)

    x = xg_ref[:, 0:LRU_WIDTH]
    gate = xg_ref[:, LRU_WIDTH:2 * LRU_WIDTH]
    xpad_ref[8:8 + ts, :] = x
    xc = cb_ref[...] + x * cw_ref[CONV_WIDTH - 1:CONV_WIDTH, :]
    for k in range(CONV_WIDTH - 1):
        back = CONV_WIDTH - 1 - k
        xc = xc + xpad_ref[8 - back:8 - back + ts, :] * cw_ref[k:k + 1, :]
    xpad_ref[0:8, :] = x[ts - 8:ts, :]

    xcb = xc.astype(BF16)
    r = _sigmoid(jnp.dot(xcb, wr_ref[...], preferred_element_type=F32) + br_ref[...])
    i = _sigmoid(jnp.dot(xcb, wi_ref[...], preferred_element_type=F32) + bi_ref[...])
    nlam = -lam_ref[...]
    softplus = jnp.maximum(nlam, 0.0) + jnp.log1p(jnp.exp(-jnp.abs(nlam)))
    log_a = (-LRU_C) * r * softplus
    a_ref[...] = jnp.exp(log_a)
    th = jnp.tanh(log_a)
    u_ref[...] = jnp.sqrt(-2.0 * th / (1.0 - th)) * (i * xc)

    def step(t, h):
        h = a_ref[pl.ds(t, 1), :] * h + u_ref[pl.ds(t, 1), :]
        hs_ref[pl.ds(t, 1), :] = h
        return h

    hcar_ref[...] = lax.fori_loop(0, ts, step, hcar_ref[...], unroll=8)

    gelu = 0.5 * gate * (1.0 + jnp.tanh(math.sqrt(2.0 / math.pi) * (gate + 0.044715 * (gate * gate * gate))))
    y = hs_ref[...] * gelu
    ms = jnp.mean(y * y, axis=-1, keepdims=True)
    o_ref[...] = (y * lax.rsqrt(ms + RMS_EPS) * g_ref[...]).astype(BF16)


def _rglru(xg, conv_w, conv_b, wr, br, wi, bi, lam, g_lru, bsz, seq):
    n = xg.shape[0]
    ts = LRU_CHUNK
    per_b = seq // ts
    row = lambda b, c: (b * per_b + c, 0)
    vec = _const_spec((1, LRU_WIDTH))
    return pl.pallas_call(
        _rglru_kernel,
        grid=(bsz, per_b),
        in_specs=[
            pl.BlockSpec((ts, 2 * LRU_WIDTH), row),
            _const_spec((CONV_WIDTH, LRU_WIDTH)), vec,
            _const_spec((LRU_WIDTH, LRU_WIDTH)), vec,
            _const_spec((LRU_WIDTH, LRU_WIDTH)), vec,
            vec, vec,
        ],
        out_specs=pl.BlockSpec((ts, LRU_WIDTH), row),
        out_shape=jax.ShapeDtypeStruct((n, LRU_WIDTH), BF16),
        scratch_shapes=[
            pltpu.VMEM((ts + 8, LRU_WIDTH), F32),
            pltpu.VMEM((ts, LRU_WIDTH), F32),
            pltpu.VMEM((ts, LRU_WIDTH), F32),
            pltpu.VMEM((ts, LRU_WIDTH), F32),
            pltpu.VMEM((1, LRU_WIDTH), F32),
        ],
        compiler_params=_cparams("parallel", "arbitrary"),
        name="rglru",
    )(xg, conv_w, conv_b, wr, br, wi, bi, lam, g_lru)


def _sb_block(q, kt, vt, tri, carry, acc, causal):
    z = jnp.dot(q, kt, preferred_element_type=F32)
    p = jnp.maximum(z, 0.0) + jnp.log2(1.0 + jnp.exp2(-jnp.abs(z)))
    if causal is not None:
        p = jnp.where(causal, p, 0.0)
    p_hi = p.astype(BF16)
    p_lo = (p - p_hi.astype(F32)).astype(BF16)
    c = jnp.dot(p_hi, tri, preferred_element_type=F32) + jnp.dot(p_lo, tri, preferred_element_type=F32)
    w = jnp.exp2(z - c - carry)
    if causal is not None:
        w = jnp.where(causal, w, 0.0)
    acc = acc + lax.dot_general(w.astype(BF16), vt, (((1,), (1,)), ((), ())), preferred_element_type=F32)
    return carry + c[:, 0:1], acc


def _sb_attn_kernel(q_ref, k_ref, v_ref, o_ref):
    tb = SB_BLOCK
    qi = pl.program_id(2)
    row = lax.broadcasted_iota(jnp.int32, (tb, tb), 0)
    col = lax.broadcasted_iota(jnp.int32, (tb, tb), 1)
    tri = (row >= col).astype(BF16)
    causal = col < row
    outs = []
    for hh in range(2):
        q = q_ref[0, hh, 0].astype(F32).T.astype(BF16)
        carry = jnp.zeros((tb, 1), F32)
        acc = jnp.zeros((tb, SB_HEAD_DIM), F32)
        carry, acc = _sb_block(q, k_ref[0, hh, qi], v_ref[0, hh, qi], tri, carry, acc, causal)

        def body(j, state, hh=hh, q=q):
            kb = qi - 1 - j
            return _sb_block(q, k_ref[0, hh, kb], v_ref[0, hh, kb], tri, state[0], state[1], None)

        carry, acc = lax.fori_loop(0, qi, body, (carry, acc))
        outs.append(acc)
    o_ref[0] = jnp.concatenate(outs, axis=-1)


def _sb_attn(qkv, bsz, seq):
    tb = SB_BLOCK
    nb = seq // tb
    hp = SB_HEADS // 2
    kv_block = (1, 2, nb, SB_HEAD_DIM, tb)
    return pl.pallas_call(
        _sb_attn_kernel,
        grid=(bsz, hp, nb),
        in_specs=[
            pl.BlockSpec((1, 2, 1, SB_HEAD_DIM, tb), lambda b, h, i: (b, h, i, 0, 0)),
            pl.BlockSpec(kv_block, lambda b, h, i: (b, hp + h, 0, 0, 0)),
            pl.BlockSpec(kv_block, lambda b, h, i: (b, 2 * hp + h, 0, 0, 0)),
        ],
        out_specs=pl.BlockSpec((1, tb, 2 * SB_HEAD_DIM), lambda b, h, i: (b, i, h)),
        out_shape=jax.ShapeDtypeStruct((bsz, seq, SB_WIDTH), F32),
        compiler_params=_cparams("parallel", "parallel", "arbitrary"),
        name="sb_attn",
    )(qkv, qkv, qkv)


def _mix_out_kernel(h_ref, ylru_ref, ysb_ref, gsb_ref, wout_ref, g_ref, b_ref, o_ref):
    ysb = ysb_ref[...]
    ms = jnp.mean(ysb * ysb, axis=-1, keepdims=True)
    ysb_n = (ysb * lax.rsqrt(ms + RMS_EPS) * gsb_ref[...]).astype(BF16)
    mix = (jnp.dot(ylru_ref[...], wout_ref[0:LRU_WIDTH, :], preferred_element_type=F32)
           + jnp.dot(ysb_n, wout_ref[LRU_WIDTH:, :], preferred_element_type=F32))
    o_ref[...] = _layer_norm(ALPHA * h_ref[...] + mix, g_ref[...], b_ref[...])


def _mix_out(h, ylru, ysb, g_sb, w_out, g, b):
    n = h.shape[0]
    tm = TOKEN_TILE
    return pl.pallas_call(
        _mix_out_kernel,
        grid=(n // tm,),
        in_specs=[
            pl.BlockSpec((tm, D_MODEL), lambda i: (i, 0)),
            pl.BlockSpec((tm, LRU_WIDTH), lambda i: (i, 0)),
            pl.BlockSpec((tm, SB_WIDTH), lambda i: (i, 0)),
            _const_spec((1, SB_WIDTH)),
            _const_spec((D_MODEL, D_MODEL)),
            _const_spec((1, D_MODEL)),
            _const_spec((1, D_MODEL)),
        ],
        out_specs=pl.BlockSpec((tm, D_MODEL), lambda i: (i, 0)),
        out_shape=jax.ShapeDtypeStruct((n, D_MODEL), F32),
        compiler_params=_cparams("parallel"),
        name="mix_out",
    )(h, ylru, ysb, g_sb, w_out, g, b)


def _mem_kv_kernel(mem_ref, wk_t_ref, wv_ref, kt_ref, v_ref):
    mb = mem_ref[0].astype(BF16)
    kt = lax.dot_general(wk_t_ref[...], mb, (((1,), (1,)), ((), ())), preferred_element_type=F32)
    kt_ref[0] = kt.astype(BF16)
    v_ref[0] = jnp.dot(mb, wv_ref[...], preferred_element_type=F32).astype(BF16)


def _mem_kv(mem, wk_t, wv):
    bsz, mlen, _ = mem.shape
    return pl.pallas_call(
        _mem_kv_kernel,
        grid=(bsz,),
        in_specs=[
            pl.BlockSpec((1, mlen, D_MODEL), lambda b: (b, 0, 0)),
            _const_spec((D_MODEL, D_MODEL)),
            _const_spec((D_MODEL, D_MODEL)),
        ],
        out_specs=[
            pl.BlockSpec((1, D_MODEL, mlen), lambda b: (b, 0, 0)),
            pl.BlockSpec((1, mlen, D_MODEL), lambda b: (b, 0, 0)),
        ],
        out_shape=[
            jax.ShapeDtypeStruct((bsz, D_MODEL, mlen), BF16),
            jax.ShapeDtypeStruct((bsz, mlen, D_MODEL), BF16),
        ],
        compiler_params=_cparams("parallel"),
        name="mem_kv",
    )(mem, wk_t, wv)


def _cross_kernel(h_ref, kt_ref, v_ref, wq_ref, wo_ref, g_ref, b_ref, o_ref):
    h = h_ref[...]
    q = jnp.dot(h.astype(BF16), wq_ref[...], preferred_element_type=F32)
    q = (q * (1.0 / math.sqrt(MEM_HEAD_DIM))).astype(BF16)
    heads = []
    for hd in range(MEM_HEADS):
        sl = slice(hd * MEM_HEAD_DIM, (hd + 1) * MEM_HEAD_DIM)
        s = jnp.dot(q[:, sl], kt_ref[0, sl, :], preferred_element_type=F32)
        e = jnp.exp(s - jnp.max(s, axis=-1, keepdims=True))
        p = e * (1.0 / jnp.sum(e, axis=-1, keepdims=True))
        heads.append(jnp.dot(p.astype(BF16), v_ref[0, :, sl], preferred_element_type=F32))
    o = jnp.concatenate(heads, axis=-1).astype(BF16)
    cross = jnp.dot(o, wo_ref[...], preferred_element_type=F32)
    o_ref[...] = _layer_norm(ALPHA * h + cross, g_ref[...], b_ref[...])


def _cross(h, kt, v, wq, wo, g, b, seq):
    n = h.shape[0]
    tm = TOKEN_TILE
    per_b = seq // tm
    mlen = v.shape[1]
    return pl.pallas_call(
        _cross_kernel,
        grid=(n // tm,),
        in_specs=[
            pl.BlockSpec((tm, D_MODEL), lambda i: (i, 0)),
            pl.BlockSpec((1, D_MODEL, mlen), lambda i: (i // per_b, 0, 0)),
            pl.BlockSpec((1, mlen, D_MODEL), lambda i: (i // per_b, 0, 0)),
            _const_spec((D_MODEL, D_MODEL)),
            _const_spec((D_MODEL, D_MODEL)),
            _const_spec((1, D_MODEL)),
            _const_spec((1, D_MODEL)),
        ],
        out_specs=pl.BlockSpec((tm, D_MODEL), lambda i: (i, 0)),
        out_shape=jax.ShapeDtypeStruct((n, D_MODEL), F32),
        compiler_params=_cparams("parallel"),
        name="cross",
    )(h, kt, v, wq, wo, g, b)


def _block_diag(w):
    nh, d, _ = w.shape
    eye = jnp.eye(nh, dtype=w.dtype)
    return (eye[:, None, :, None] * w[:, :, None, :]).reshape(nh * d, nh * d)


def kernel(x, mem, ffn1_w13, ffn1_w2, ln1_g, ln1_b, w_in, conv_w, conv_b, w_rgate, b_rgate, w_igate, b_igate, lru_lambda, g_lru, g_sb, w_out, ln2_g, ln2_b, mem_wq, mem_wkv, mem_wo, ln3_g, ln3_b, ffn2_w13, ffn2_w2, ln4_g, ln4_b):
    bsz, seq, _ = x.shape
    assert seq % TOKEN_TILE == 0 and seq % LRU_CHUNK == 0 and TOKEN_TILE % SB_BLOCK == 0
    h = x.reshape(bsz * seq, D_MODEL)
    vec = lambda v: v.reshape(1, -1)
    for l in range(DEPTH):
        h = _ffn_ln(h, ffn1_w13[l].astype(BF16), ffn1_w2[l].astype(BF16), vec(ln1_g[l]), vec(ln1_b[l]))
        wxg = w_in[l][:, :2 * LRU_WIDTH].astype(BF16)
        wqkv_t = w_in[l][:, 2 * LRU_WIDTH:].T.astype(BF16)
        xg, qkv = _mixer_in(h, wxg, wqkv_t, bsz, seq)
        ylru = _rglru(xg, conv_w[l], vec(conv_b[l]),
                      _block_diag(w_rgate[l]).astype(BF16), vec(b_rgate[l]),
                      _block_diag(w_igate[l]).astype(BF16), vec(b_igate[l]),
                      vec(lru_lambda[l]), vec(g_lru[l]), bsz, seq)
        ysb = _sb_attn(qkv, bsz, seq).reshape(bsz * seq, SB_WIDTH)
        h = _mix_out(h, ylru, ysb, vec(g_sb[l]), w_out[l].astype(BF16), vec(ln2_g[l]), vec(ln2_b[l]))
        kt, v = _mem_kv(mem, mem_wkv[l][:, :D_MODEL].T.astype(BF16), mem_wkv[l][:, D_MODEL:].astype(BF16))
        h = _cross(h, kt, v, mem_wq[l].astype(BF16), mem_wo[l].astype(BF16), vec(ln3_g[l]), vec(ln3_b[l]), seq)
        h = _ffn_ln(h, ffn2_w13[l].astype(BF16), ffn2_w2[l].astype(BF16), vec(ln4_g[l]), vec(ln4_b[l]))
    return h.reshape(bsz, seq, D_MODEL)
```

```python
import functools
import math

import jax
import jax.numpy as jnp
from jax import lax
from jax.experimental import pallas as pl
from jax.experimental.pallas import tpu as pltpu

F32 = jnp.float32
BF16 = jnp.bfloat16

D_MODEL = 1024
DEPTH = 1
LRU_WIDTH = D_MODEL // 2
LRU_HEADS = 8
LRU_HEAD_DIM = LRU_WIDTH // LRU_HEADS
CONV_WIDTH = 4
LRU_C = 8.0
SB_WIDTH = D_MODEL - LRU_WIDTH
SB_HEADS = 8
SB_HEAD_DIM = SB_WIDTH // SB_HEADS
D_FF = 2688
MEM_HEADS = 4
MEM_HEAD_DIM = D_MODEL // MEM_HEADS
ALPHA = (2 * DEPTH) ** 0.25
LN_EPS = 1e-5
RMS_EPS = 1e-6

LOG2E = 1.4426950408889634
LANES = 128

VMEM_LIMIT_BYTES = 56 * 1024 * 1024

TOKEN_TILE = 512
FF_CHUNKS = (1280, 1408)
SB_BLOCK = 256
SB_QTILE = 1024
LRU_CHUNK = 512


def _cparams(*sem):
    return pltpu.CompilerParams(dimension_semantics=sem, vmem_limit_bytes=VMEM_LIMIT_BYTES)


def _const_spec(shape):
    nd = len(shape)
    return pl.BlockSpec(shape, lambda *_: (0,) * nd, pipeline_mode=pl.Buffered(1))


def _layer_norm(r, g, b):
    mu = jnp.mean(r, axis=-1, keepdims=True)
    d = r - mu
    var = jnp.mean(d * d, axis=-1, keepdims=True)
    return d * lax.rsqrt(var + LN_EPS) * g + b


def _sigmoid(x):
    return 1.0 / (1.0 + jnp.exp(-x))


def _ffn_ln_kernel(x_ref, w13_ref, w2_ref, g_ref, b_ref, o_ref):
    x = x_ref[...]
    xb = x.astype(BF16)
    y = None
    off = 0
    for width in FF_CHUNKS:
        gate = jnp.dot(xb, w13_ref[:, off:off + width], preferred_element_type=F32)
        up = jnp.dot(xb, w13_ref[:, D_FF + off:D_FF + off + width], preferred_element_type=F32)
        act = (gate * _sigmoid(gate) * up).astype(BF16)
        part = jnp.dot(act, w2_ref[off:off + width, :], preferred_element_type=F32)
        y = part if y is None else y + part
        off += width
    o_ref[...] = _layer_norm(ALPHA * x + 0.5 * y, g_ref[...], b_ref[...])


def _ffn_ln(x, w13, w2, g, b):
    n = x.shape[0]
    tm = TOKEN_TILE
    return pl.pallas_call(
        _ffn_ln_kernel,
        grid=(n // tm,),
        in_specs=[
            pl.BlockSpec((tm, D_MODEL), lambda i: (i, 0)),
            _const_spec((D_MODEL, 2 * D_FF)),
            _const_spec((D_FF, D_MODEL)),
            _const_spec((1, D_MODEL)),
            _const_spec((1, D_MODEL)),
        ],
        out_specs=pl.BlockSpec((tm, D_MODEL), lambda i: (i, 0)),
        out_shape=jax.ShapeDtypeStruct((n, D_MODEL), F32),
        compiler_params=_cparams("parallel"),
        name="ffn_ln",
    )(x, w13, w2, g, b)


def _mixer_in_kernel(h_ref, wxg_ref, wqkv_t_ref, xg_ref, qkv_ref):
    hb = h_ref[...].astype(BF16)
    xg_ref[...] = jnp.dot(hb, wxg_ref[...], preferred_element_type=F32)
    t = lax.dot_general(wqkv_t_ref[...], hb, (((1,), (1,)), ((), ())), preferred_element_type=F32)
    tm = hb.shape[0]
    nh = 3 * SB_HEADS
    for c in range(tm // SB_BLOCK):
        blk = t[:, c * SB_BLOCK:(c + 1) * SB_BLOCK]
        for j in range(nh):
            rows = blk[j * SB_HEAD_DIM:(j + 1) * SB_HEAD_DIM, :]
            if j < SB_HEADS:
                rows = rows * (LOG2E / math.sqrt(SB_HEAD_DIM))
            qkv_ref[0, j, c] = rows.astype(BF16)


def _mixer_in(h, wxg, wqkv_t, bsz, seq):
    n = h.shape[0]
    tm = TOKEN_TILE
    per_b = seq // tm
    nbt = tm // SB_BLOCK
    nh = 3 * SB_HEADS
    return pl.pallas_call(
        _mixer_in_kernel,
        grid=(n // tm,),
        in_specs=[
            pl.BlockSpec((tm, D_MODEL), lambda i: (i, 0)),
            _const_spec((D_MODEL, 2 * LRU_WIDTH)),
            _const_spec((3 * SB_WIDTH, D_MODEL)),
        ],
        out_specs=[
            pl.BlockSpec((tm, 2 * LRU_WIDTH), lambda i: (i, 0)),
            pl.BlockSpec((1, nh, nbt, SB_HEAD_DIM, SB_BLOCK),
                         lambda i: (i // per_b, 0, i % per_b, 0, 0)),
        ],
        out_shape=[
            jax.ShapeDtypeStruct((n, 2 * LRU_WIDTH), F32),
            jax.ShapeDtypeStruct((bsz, nh, seq // SB_BLOCK, SB_HEAD_DIM, SB_BLOCK), BF16),
        ],
        compiler_params=_cparams("parallel"),
        name="mixer_in",
    )(h, wxg, wqkv_t)


def _rglru_kernel(xg_ref, cw_ref, cb_ref, wr_ref, br_ref, wi_ref, bi_ref, lam_ref, g_ref,
                  o_ref, xpad_ref, a_ref, u_ref, hs_ref, hcar_ref):
    ts = LRU_CHUNK
    c = pl.program_id(1)

    @pl.when(c == 0)
    def _():
        xpad_ref[0:8, :] = jnp.zeros((8, LRU_WIDTH), F32)
        hcar_ref[...] = jnp.zeros_like(hcar_ref)

    x = xg_ref[:, 0:LRU_WIDTH]
    gate = xg_ref[:, LRU_WIDTH:2 * LRU_WIDTH]
    xpad_ref[8:8 + ts, :] = x
    xc = cb_ref[...] + x * cw_ref[CONV_WIDTH - 1:CONV_WIDTH, :]
    for k in range(CONV_WIDTH - 1):
        back = CONV_WIDTH - 1 - k
        xc = xc + xpad_ref[8 - back:8 - back + ts, :] * cw_ref[k:k + 1, :]
    xpad_ref[0:8, :] = x[ts - 8:ts, :]

    xcb = xc.astype(BF16)
    r = _sigmoid(jnp.dot(xcb, wr_ref[...], preferred_element_type=F32) + br_ref[...])
    i = _sigmoid(jnp.dot(xcb, wi_ref[...], preferred_element_type=F32) + bi_ref[...])
    nlam = -lam_ref[...]
    softplus = jnp.maximum(nlam, 0.0) + jnp.log1p(jnp.exp(-jnp.abs(nlam)))
    log_a = (-LRU_C) * r * softplus
    a_ref[...] = jnp.exp(log_a)
    th = jnp.tanh(log_a)
    u_ref[...] = jnp.sqrt(-2.0 * th / (1.0 - th)) * (i * xc)

    def step(t, h):
        h = a_ref[pl.ds(t, 1), :] * h + u_ref[pl.ds(t, 1), :]
        hs_ref[pl.ds(t, 1), :] = h
        return h

    hcar_ref[...] = lax.fori_loop(0, ts, step, hcar_ref[...], unroll=8)

    gelu = 0.5 * gate * (1.0 + jnp.tanh(math.sqrt(2.0 / math.pi) * (gate + 0.044715 * (gate * gate * gate))))
    y = hs_ref[...] * gelu
    ms = jnp.mean(y * y, axis=-1, keepdims=True)
    o_ref[...] = (y * lax.rsqrt(ms + RMS_EPS) * g_ref[...]).astype(BF16)


def _rglru(xg, conv_w, conv_b, wr, br, wi, bi, lam, g_lru, bsz, seq):
    n = xg.shape[0]
    ts = LRU_CHUNK
    per_b = seq // ts
    row = lambda b, c: (b * per_b + c, 0)
    vec = _const_spec((1, LRU_WIDTH))
    return pl.pallas_call(
        _rglru_kernel,
        grid=(bsz, per_b),
        in_specs=[
            pl.BlockSpec((ts, 2 * LRU_WIDTH), row),
            _const_spec((CONV_WIDTH, LRU_WIDTH)), vec,
            _const_spec((LRU_WIDTH, LRU_WIDTH)), vec,
            _const_spec((LRU_WIDTH, LRU_WIDTH)), vec,
            vec, vec,
        ],
        out_specs=pl.BlockSpec((ts, LRU_WIDTH), row),
        out_shape=jax.ShapeDtypeStruct((n, LRU_WIDTH), BF16),
        scratch_shapes=[
            pltpu.VMEM((ts + 8, LRU_WIDTH), F32),
            pltpu.VMEM((ts, LRU_WIDTH), F32),
            pltpu.VMEM((ts, LRU_WIDTH), F32),
            pltpu.VMEM((ts, LRU_WIDTH), F32),
            pltpu.VMEM((1, LRU_WIDTH), F32),
        ],
        compiler_params=_cparams("parallel", "arbitrary"),
        name="rglru",
    )(xg, conv_w, conv_b, wr, br, wi, bi, lam, g_lru)


def _sb_head_block(q, kt, vt_pad, tri, carry, causal):
    z = jnp.dot(q, kt, preferred_element_type=F32)
    neg_abs = lax.bitcast_convert_type(lax.bitcast_convert_type(z, jnp.int32) | jnp.int32(-2 ** 31), F32)
    p = jnp.maximum(z, 0.0) + jnp.log2(1.0 + jnp.exp2(neg_abs))
    if causal is not None:
        p = jnp.where(causal, p, 0.0)
    c = jnp.dot(p.astype(BF16), tri, preferred_element_type=F32)
    w =jnp.exp2(z - c - jnp.concatenate([carry] * (z.shape[1] // LANES), axis=1))
    if causal is not None:
        w = jnp.where(causal, w, 0.0)
    pv = lax.dot_general(w.astype(BF16), vt_pad, (((1,), (1,)), ((), ())), preferred_element_type=F32)
    return carry + jnp.broadcast_to(c[:, 0:1], carry.shape), pv


def _sb_attn_kernel(q_ref, k_ref, v_ref, o_ref, acc_ref):
    tb, tq = SB_BLOCK, SB_QTILE
    r = tq // tb
    qi = pl.program_id(2)
    tri_row = lax.broadcasted_iota(jnp.int32, (tb, tb), 0)
    tri_col = lax.broadcasted_iota(jnp.int32, (tb, tb), 1)
    tri = (tri_row >= tri_col).astype(BF16)
    zpad = jnp.zeros((SB_HEAD_DIM, tb), BF16)
    qs = [jnp.concatenate([q_ref[0, hh, c].astype(F32).T for c in range(r)], axis=0).astype(BF16)
          for hh in range(2)]

    def step(kb, carries, lo, causal):
        new, total = [], None
        for hh in range(2):
            v = v_ref[0, hh, kb]
            vt_pad = jnp.concatenate([v, zpad] if hh == 0 else [zpad, v], axis=0)
            cnew, pv = _sb_head_block(qs[hh][lo:], k_ref[0, hh, kb], vt_pad, tri, carries[hh][lo:], causal)
            new.append(cnew if lo == 0 else jnp.concatenate([carries[hh][:lo], cnew], axis=0))
            total = pv if total is None else total + pv
        acc_ref[lo:, :] += total
        return tuple(new)

    acc_ref[...] = jnp.zeros_like(acc_ref)
    carries = (jnp.zeros((tq, LANES), F32), jnp.zeros((tq, LANES), F32))
    for m in reversed(range(r)):
        lo = m * tb
        row = lax.broadcasted_iota(jnp.int32, (tq - lo, tb), 0)
        col = lax.broadcasted_iota(jnp.int32, (tq - lo, tb), 1)
        carries = step(r * qi + m, carries, lo, col < row)

    def body(j, carries):
        for m in reversed(range(r)):
            carries = step(r * (qi - 1 - j) + m, carries, 0, None)
        return carries

    lax.fori_loop(0, qi, body, carries)
    o_ref[0] = acc_ref[...]


def _sb_attn(qkv, bsz, seq):
    tb, tq = SB_BLOCK, SB_QTILE
    nb = seq // tb
    r = tq // tb
    hp = SB_HEADS // 2
    kv_block = (1, 2, nb, SB_HEAD_DIM, tb)
    return pl.pallas_call(
        _sb_attn_kernel,
        grid=(bsz, hp, seq // tq),
        in_specs=[
            pl.BlockSpec((1, 2, r, SB_HEAD_DIM, tb), lambda b, h, i: (b, h, i, 0, 0)),
            pl.BlockSpec(kv_block, lambda b, h, i: (b, hp + h, 0, 0, 0)),
            pl.BlockSpec(kv_block, lambda b, h, i: (b, 2 * hp + h, 0, 0, 0)),
        ],
        out_specs=pl.BlockSpec((1, tq, 2 * SB_HEAD_DIM), lambda b, h, i: (b, i, h)),
        out_shape=jax.ShapeDtypeStruct((bsz, seq, SB_WIDTH), F32),
        scratch_shapes=[pltpu.VMEM((tq, 2 * SB_HEAD_DIM), F32)],
        compiler_params=_cparams("parallel", "parallel", "arbitrary"),
        name="sb_attn",
    )(qkv, qkv, qkv)


def _mix_out_kernel(h_ref, ylru_ref, ysb_ref, gsb_ref, wout_ref, g_ref, b_ref, o_ref):
    ysb = ysb_ref[...]
    ms = jnp.mean(ysb * ysb, axis=-1, keepdims=True)
    ysb_n = (ysb * lax.rsqrt(ms + RMS_EPS) * gsb_ref[...]).astype(BF16)
    mix = (jnp.dot(ylru_ref[...], wout_ref[0:LRU_WIDTH, :], preferred_element_type=F32)
           + jnp.dot(ysb_n, wout_ref[LRU_WIDTH:, :], preferred_element_type=F32))
    o_ref[...] = _layer_norm(ALPHA * h_ref[...] + mix, g_ref[...], b_ref[...])


def _mix_out(h, ylru, ysb, g_sb, w_out, g, b):
    n = h.shape[0]
    tm = TOKEN_TILE
    return pl.pallas_call(
        _mix_out_kernel,
        grid=(n // tm,),
        in_specs=[
            pl.BlockSpec((tm, D_MODEL), lambda i: (i, 0)),
            pl.BlockSpec((tm, LRU_WIDTH), lambda i: (i, 0)),
            pl.BlockSpec((tm, SB_WIDTH), lambda i: (i, 0)),
            _const_spec((1, SB_WIDTH)),
            _const_spec((D_MODEL, D_MODEL)),
            _const_spec((1, D_MODEL)),
            _const_spec((1, D_MODEL)),
        ],
        out_specs=pl.BlockSpec((tm, D_MODEL), lambda i: (i, 0)),
        out_shape=jax.ShapeDtypeStruct((n, D_MODEL), F32),
        compiler_params=_cparams("parallel"),
        name="mix_out",
    )(h, ylru, ysb, g_sb, w_out, g, b)


def _mem_kv_kernel(mem_ref, wk_t_ref, wv_ref, kt_ref, v_ref):
    mb = mem_ref[0].astype(BF16)
    kt = lax.dot_general(wk_t_ref[...], mb, (((1,), (1,)), ((), ())), preferred_element_type=F32)
    kt_ref[0] = kt.astype(BF16)
    v_ref[0] = jnp.dot(mb, wv_ref[...], preferred_element_type=F32).astype(BF16)


def _mem_kv(mem, wk_t, wv):
    bsz, mlen, _ = mem.shape
    return pl.pallas_call(
        _mem_kv_kernel,
        grid=(bsz,),
        in_specs=[
            pl.BlockSpec((1, mlen, D_MODEL), lambda b: (b, 0, 0)),
            _const_spec((D_MODEL, D_MODEL)),
            _const_spec((D_MODEL, D_MODEL)),
        ],
        out_specs=[
            pl.BlockSpec((1, D_MODEL, mlen), lambda b: (b, 0, 0)),
            pl.BlockSpec((1, mlen, D_MODEL), lambda b: (b, 0, 0)),
        ],
        out_shape=[
            jax.ShapeDtypeStruct((bsz, D_MODEL, mlen), BF16),
            jax.ShapeDtypeStruct((bsz, mlen, D_MODEL), BF16),
        ],
        compiler_params=_cparams("parallel"),
        name="mem_kv",
    )(mem, wk_t, wv)


def _cross_kernel(h_ref, kt_ref, v_ref, wq_ref, wo_ref, g_ref, b_ref, o_ref):
    h = h_ref[...]
    q = jnp.dot(h.astype(BF16), wq_ref[...], preferred_element_type=F32)
    q = (q * (1.0 / math.sqrt(MEM_HEAD_DIM))).astype(BF16)
    heads = []
    for hd in range(MEM_HEADS):
        sl = slice(hd * MEM_HEAD_DIM, (hd + 1) * MEM_HEAD_DIM)
        s = jnp.dot(q[:, sl], kt_ref[0, sl, :], preferred_element_type=F32)
        e = jnp.exp(s - jnp.max(s, axis=-1, keepdims=True))
        p = e * (1.0 / jnp.sum(e, axis=-1, keepdims=True))
        heads.append(jnp.dot(p.astype(BF16), v_ref[0, :, sl], preferred_element_type=F32))
    o = jnp.concatenate(heads, axis=-1).astype(BF16)
    cross = jnp.dot(o, wo_ref[...], preferred_element_type=F32)
    o_ref[...] = _layer_norm(ALPHA * h + cross, g_ref[...], b_ref[...])


def _cross(h, kt, v, wq, wo, g, b, seq):
    n = h.shape[0]
    tm = TOKEN_TILE
    per_b = seq // tm
    mlen = v.shape[1]
    return pl.pallas_call(
        _cross_kernel,
        grid=(n // tm,),
        in_specs=[
            pl.BlockSpec((tm, D_MODEL), lambda i: (i, 0)),
            pl.BlockSpec((1, D_MODEL, mlen), lambda i: (i // per_b, 0, 0)),
            pl.BlockSpec((1, mlen, D_MODEL), lambda i: (i // per_b, 0, 0)),
            _const_spec((D_MODEL, D_MODEL)),
            _const_spec((D_MODEL, D_MODEL)),
            _const_spec((1, D_MODEL)),
            _const_spec((1, D_MODEL)),
        ],
        out_specs=pl.BlockSpec((tm, D_MODEL), lambda i: (i, 0)),
        out_shape=jax.ShapeDtypeStruct((n, D_MODEL), F32),
        compiler_params=_cparams("parallel"),
        name="cross",
    )(h, kt, v, wq, wo, g, b)


def _block_diag(w):
    nh, d, _ = w.shape
    eye = jnp.eye(nh, dtype=w.dtype)
    return (eye[:, None, :, None] * w[:, :, None, :]).reshape(nh * d, nh * d)


def kernel(x, mem, ffn1_w13, ffn1_w2, ln1_g, ln1_b, w_in, conv_w, conv_b, w_rgate, b_rgate, w_igate, b_igate, lru_lambda, g_lru, g_sb, w_out, ln2_g, ln2_b, mem_wq, mem_wkv, mem_wo, ln3_g, ln3_b, ffn2_w13, ffn2_w2, ln4_g, ln4_b):
    bsz, seq, _ = x.shape
    assert seq % TOKEN_TILE == 0 and seq % LRU_CHUNK == 0 and seq % SB_QTILE == 0
    assert TOKEN_TILE % SB_BLOCK == 0 and SB_QTILE % SB_BLOCK == 0
    h = x.reshape(bsz * seq, D_MODEL)
    vec = lambda v: v.reshape(1, -1)
    for l in range(DEPTH):
        h = _ffn_ln(h, ffn1_w13[l].astype(BF16), ffn1_w2[l].astype(BF16), vec(ln1_g[l]), vec(ln1_b[l]))
        wxg = w_in[l][:, :2 * LRU_WIDTH].astype(BF16)
        wqkv_t = w_in[l][:, 2 * LRU_WIDTH:].T.astype(BF16)
        xg, qkv = _mixer_in(h, wxg, wqkv_t, bsz, seq)
        ylru = _rglru(xg, conv_w[l], vec(conv_b[l]),
                      _block_diag(w_rgate[l]).astype(BF16), vec(b_rgate[l]),
                      _block_diag(w_igate[l]).astype(BF16), vec(b_igate[l]),
                      vec(lru_lambda[l]), vec(g_lru[l]), bsz, seq)
        ysb = _sb_attn(qkv, bsz, seq).reshape(bsz * seq, SB_WIDTH)
        h = _mix_out(h, ylru, ysb, vec(g_sb[l]), w_out[l].astype(BF16), vec(ln2_g[l]), vec(ln2_b[l]))
        kt, v = _mem_kv(mem, mem_wkv[l][:, :D_MODEL].T.astype(BF16), mem_wkv[l][:, D_MODEL:].astype(BF16))
        h = _cross(h, kt, v, mem_wq[l].astype(BF16), mem_wo[l].astype(BF16), vec(ln3_g[l]), vec(ln3_b[l]), seq)
        h = _ffn_ln(h, ffn2_w13[l].astype(BF16), ffn2_w2[l].astype(BF16), vec(ln4_g[l]), vec(ln4_b[l]))
    return h.reshape(bsz, seq, D_MODEL)
```

```python
import functools
import math

import jax
import jax.numpy as jnp
from jax import lax
from jax.experimental import pallas as pl
from jax.experimental.pallas import tpu as pltpu

F32 = jnp.float32
BF16 = jnp.bfloat16

D_MODEL = 1024
DEPTH = 1
LRU_WIDTH = D_MODEL // 2
LRU_HEADS = 8
LRU_HEAD_DIM = LRU_WIDTH // LRU_HEADS
CONV_WIDTH = 4
LRU_C = 8.0
SB_WIDTH = D_MODEL - LRU_WIDTH
SB_HEADS = 8
SB_HEAD_DIM = SB_WIDTH // SB_HEADS
D_FF = 2688
MEM_HEADS = 4
MEM_HEAD_DIM = D_MODEL // MEM_HEADS
ALPHA = (2 * DEPTH) ** 0.25
LN_EPS = 1e-5
RMS_EPS = 1e-6

LOG2E = 1.4426950408889634
LANES = 128

VMEM_LIMIT_BYTES = 56 * 1024 * 1024

TOKEN_TILE = 512
FF_CHUNKS = (1280, 1408)
SB_BLOCK = 256
SB_QTILE = 512
SB_DEAD_LOG2 = 160.0
LRU_CHUNK = 512


def _cparams(*sem):
    return pltpu.CompilerParams(dimension_semantics=sem, vmem_limit_bytes=VMEM_LIMIT_BYTES)


def _const_spec(shape):
    nd = len(shape)
    return pl.BlockSpec(shape, lambda *_: (0,) * nd, pipeline_mode=pl.Buffered(1))


def _layer_norm(r, g, b):
    mu = jnp.mean(r, axis=-1, keepdims=True)
    d = r - mu
    var = jnp.mean(d * d, axis=-1, keepdims=True)
    return d * lax.rsqrt(var + LN_EPS) * g + b


def _sigmoid(x):
    return 1.0 / (1.0 + jnp.exp(-x))


def _ffn_ln_kernel(x_ref, w13_ref, w2_ref, g_ref, b_ref, o_ref):
    x = x_ref[...]
    xb = x.astype(BF16)
    y = None
    off = 0
    for width in FF_CHUNKS:
        gate = jnp.dot(xb, w13_ref[:, off:off + width], preferred_element_type=F32)
        up = jnp.dot(xb, w13_ref[:, D_FF + off:D_FF + off + width], preferred_element_type=F32)
        act = (gate * _sigmoid(gate) * up).astype(BF16)
        part = jnp.dot(act, w2_ref[off:off + width, :], preferred_element_type=F32)
        y = part if y is None else y + part
        off += width
    o_ref[...] = _layer_norm(ALPHA * x + 0.5 * y, g_ref[...], b_ref[...])


def _ffn_ln(x, w13, w2, g, b):
    n = x.shape[0]
    tm = TOKEN_TILE
    return pl.pallas_call(
        _ffn_ln_kernel,
        grid=(n // tm,),
        in_specs=[
            pl.BlockSpec((tm, D_MODEL), lambda i: (i, 0)),
            _const_spec((D_MODEL, 2 * D_FF)),
            _const_spec((D_FF, D_MODEL)),
            _const_spec((1, D_MODEL)),
            _const_spec((1, D_MODEL)),
        ],
        out_specs=pl.BlockSpec((tm, D_MODEL), lambda i: (i, 0)),
        out_shape=jax.ShapeDtypeStruct((n, D_MODEL), F32),
        compiler_params=_cparams("parallel"),
        name="ffn_ln",
    )(x, w13, w2, g, b)


def _mixer_in_kernel(h_ref, wxg_ref, wqkv_t_ref, xg_ref, qkv_ref):
    hb = h_ref[...].astype(BF16)
    xg_ref[...] = jnp.dot(hb, wxg_ref[...], preferred_element_type=F32)
    t = lax.dot_general(wqkv_t_ref[...], hb, (((1,), (1,)), ((), ())), preferred_element_type=F32)
    tm = hb.shape[0]
    nh = 3 * SB_HEADS
    for c in range(tm // SB_BLOCK):
        blk = t[:, c * SB_BLOCK:(c + 1) * SB_BLOCK]
        for j in range(nh):
            rows = blk[j * SB_HEAD_DIM:(j + 1) * SB_HEAD_DIM, :]
            if j < SB_HEADS:
                rows = rows * (LOG2E / math.sqrt(SB_HEAD_DIM))
            qkv_ref[0, j, c] = rows.astype(BF16)


def _mixer_in(h, wxg, wqkv_t, bsz, seq):
    n = h.shape[0]
    tm = TOKEN_TILE
    per_b = seq // tm
    nbt = tm // SB_BLOCK
    nh = 3 * SB_HEADS
    return pl.pallas_call(
        _mixer_in_kernel,
        grid=(n // tm,),
        in_specs=[
            pl.BlockSpec((tm, D_MODEL), lambda i: (i, 0)),
            _const_spec((D_MODEL, 2 * LRU_WIDTH)),
            _const_spec((3 * SB_WIDTH, D_MODEL)),
        ],
        out_specs=[
            pl.BlockSpec((tm, 2 * LRU_WIDTH), lambda i: (i, 0)),
            pl.BlockSpec((1, nh, nbt, SB_HEAD_DIM, SB_BLOCK),
                         lambda i: (i // per_b, 0, i % per_b, 0, 0)),
        ],
        out_shape=[
            jax.ShapeDtypeStruct((n, 2 * LRU_WIDTH), F32),
            jax.ShapeDtypeStruct((bsz, nh, seq // SB_BLOCK, SB_HEAD_DIM, SB_BLOCK), BF16),
        ],
        compiler_params=_cparams("parallel"),
        name="mixer_in",
    )(h, wxg, wqkv_t)


def _rglru_kernel(xg_ref, cw_ref, cb_ref, wr_ref, br_ref, wi_ref, bi_ref, lam_ref, g_ref,
                  o_ref, xpad_ref, a_ref, u_ref, hs_ref, hcar_ref):
    ts = LRU_CHUNK
    c = pl.program_id(1)

    @pl.when(c == 0)
    def _():
        xpad_ref[0:8, :] = jnp.zeros((8, LRU_WIDTH), F32)
        hcar_ref[...] = jnp.zeros_like(hcar_ref)

    x = xg_ref[:, 0:LRU_WIDTH]
    gate = xg_ref[:, LRU_WIDTH:2 * LRU_WIDTH]
    xpad_ref[8:8 + ts, :] = x
    xc = cb_ref[...] + x * cw_ref[CONV_WIDTH - 1:CONV_WIDTH, :]
    for k in range(CONV_WIDTH - 1):
        back = CONV_WIDTH - 1 - k
        xc = xc + xpad_ref[8 - back:8 - back + ts, :] * cw_ref[k:k + 1, :]
    xpad_ref[0:8, :] = x[ts - 8:ts, :]

    xcb = xc.astype(BF16)
    r = _sigmoid(jnp.dot(xcb, wr_ref[...], preferred_element_type=F32) + br_ref[...])
    i = _sigmoid(jnp.dot(xcb, wi_ref[...], preferred_element_type=F32) + bi_ref[...])
    nlam = -lam_ref[...]
    softplus = jnp.maximum(nlam, 0.0) + jnp.log1p(jnp.exp(-jnp.abs(nlam)))
    log_a = (-LRU_C) * r * softplus
    a_ref[...] = jnp.exp(log_a)
    th = jnp.tanh(log_a)
    u_ref[...] = jnp.sqrt(-2.0 * th / (1.0 - th)) * (i * xc)

    def step(t, h):
        h = a_ref[pl.ds(t, 1), :] * h + u_ref[pl.ds(t, 1), :]
        hs_ref[pl.ds(t, 1), :] = h
        return h

    hcar_ref[...] = lax.fori_loop(0, ts, step, hcar_ref[...], unroll=8)

    gelu = 0.5 * gate * (1.0 + jnp.tanh(math.sqrt(2.0 / math.pi) * (gate + 0.044715 * (gate * gate * gate))))
    y = hs_ref[...] * gelu
    ms = jnp.mean(y * y, axis=-1, keepdims=True)
    o_ref[...] = (y * lax.rsqrt(ms + RMS_EPS) * g_ref[...]).astype(BF16)


def _rglru(xg, conv_w, conv_b, wr, br, wi, bi, lam, g_lru, bsz, seq):
    n = xg.shape[0]
    ts = LRU_CHUNK
    per_b = seq // ts
    row = lambda b, c: (b * per_b + c, 0)
    vec = _const_spec((1, LRU_WIDTH))
    return pl.pallas_call(
        _rglru_kernel,
        grid=(bsz, per_b),
        in_specs=[
            pl.BlockSpec((ts, 2 * LRU_WIDTH), row),
            _const_spec((CONV_WIDTH, LRU_WIDTH)), vec,
            _const_spec((LRU_WIDTH, LRU_WIDTH)), vec,
            _const_spec((LRU_WIDTH, LRU_WIDTH)), vec,
            vec, vec,
        ],
        out_specs=pl.BlockSpec((ts, LRU_WIDTH), row),
        out_shape=jax.ShapeDtypeStruct((n, LRU_WIDTH), BF16),
        scratch_shapes=[
            pltpu.VMEM((ts + 8, LRU_WIDTH), F32),
            pltpu.VMEM((ts, LRU_WIDTH), F32),
            pltpu.VMEM((ts, LRU_WIDTH), F32),
            pltpu.VMEM((ts, LRU_WIDTH), F32),
            pltpu.VMEM((1, LRU_WIDTH), F32),
        ],
        compiler_params=_cparams("parallel", "arbitrary"),
        name="rglru",
    )(xg, conv_w, conv_b, wr, br, wi, bi, lam, g_lru)


def _sb_head_block(q, kt, vt_pad, tri, carry, causal):
    z = jnp.dot(q, kt, preferred_element_type=F32)
    neg_abs = lax.bitcast_convert_type(lax.bitcast_convert_type(z, jnp.int32) | jnp.int32(-2 ** 31), F32)
    p = jnp.maximum(z, 0.0) + jnp.log2(1.0 + jnp.exp2(neg_abs))
    if causal is not None:
        p = jnp.where(causal, p, 0.0)
    c = jnp.dot(p.astype(BF16), tri, preferred_element_type=F32)
    d = jnp.minimum(z - c, 0.0)
    w = jnp.exp2(d - jnp.concatenate([carry] * (z.shape[1] // LANES), axis=1))
    if causal is not None:
        w = jnp.where(causal, w, 0.0)
    pv = lax.dot_general(w.astype(BF16), vt_pad, (((1,), (1,)), ((), ())), preferred_element_type=F32)
    return carry + jnp.broadcast_to(c[:, 0:1], carry.shape), pv


def _sb_attn_kernel(q_ref, k_ref, v_ref, o_ref, acc_ref):
    tb, tq = SB_BLOCK, SB_QTILE
    r = tq // tb
    qi = pl.program_id(2)
    tri_row = lax.broadcasted_iota(jnp.int32, (tb, tb), 0)
    tri_col = lax.broadcasted_iota(jnp.int32, (tb, tb), 1)
    tri = (tri_row >= tri_col).astype(BF16)
    zpad = jnp.zeros((SB_HEAD_DIM, tb), BF16)
    qs = [jnp.concatenate([q_ref[0, hh, c].astype(F32).T for c in range(r)], axis=0).astype(BF16)
          for hh in range(2)]

    def step(kb, carries, lo, causal):
        new, total = [], None
        for hh in range(2):
            v = v_ref[0, hh, kb]
            vt_pad = jnp.concatenate([v, zpad] if hh == 0 else [zpad, v], axis=0)
            cnew, pv = _sb_head_block(qs[hh][lo:], k_ref[0, hh, kb], vt_pad, tri, carries[hh][lo:], causal)
            new.append(cnew if lo == 0 else jnp.concatenate([carries[hh][:lo], cnew], axis=0))
            total = pv if total is None else total + pv
        acc_ref[lo:, :] += total
        return tuple(new)

    acc_ref[...] = jnp.zeros_like(acc_ref)
    carries = (jnp.zeros((tq, LANES), F32), jnp.zeros((tq, LANES), F32))
    for m in reversed(range(r)):
        lo = m * tb
        row = lax.broadcasted_iota(jnp.int32, (tq - lo, tb), 0)
        col = lax.broadcasted_iota(jnp.int32, (tq - lo, tb), 1)
        carries = step(r * qi + m, carries, lo, col < row)

    def body(state):
        j, carries = state[0], state[1:]
        for m in reversed(range(r)):
            carries = step(r * (qi - 1 - j) + m, carries, 0, None)
        return (j + 1,) + carries

    def more(state):
        j, c0, c1 = state
        return jnp.logical_and(j < qi, jnp.min(jnp.minimum(c0, c1)) < SB_DEAD_LOG2)

    lax.while_loop(more, body, (jnp.int32(0),) + carries)
    o_ref[0] = acc_ref[...]


def _sb_attn(qkv, bsz, seq):
    tb, tq = SB_BLOCK, SB_QTILE
    nb = seq // tb
    r = tq // tb
    hp = SB_HEADS // 2
    kv_block = (1, 2, nb, SB_HEAD_DIM, tb)
    return pl.pallas_call(
        _sb_attn_kernel,
        grid=(bsz, hp, seq // tq),
        in_specs=[
            pl.BlockSpec((1, 2, r, SB_HEAD_DIM, tb), lambda b, h, i: (b, h, i, 0, 0)),
            pl.BlockSpec(kv_block, lambda b, h, i: (b, hp + h, 0, 0, 0)),
            pl.BlockSpec(kv_block, lambda b, h, i: (b, 2 * hp + h, 0, 0, 0)),
        ],
        out_specs=pl.BlockSpec((1, tq, 2 * SB_HEAD_DIM), lambda b, h, i: (b, i, h)),
        out_shape=jax.ShapeDtypeStruct((bsz, seq, SB_WIDTH), F32),
        scratch_shapes=[pltpu.VMEM((tq, 2 * SB_HEAD_DIM), F32)],
        compiler_params=_cparams("parallel", "parallel", "arbitrary"),
        name="sb_attn",
    )(qkv, qkv, qkv)


def _mix_out_kernel(h_ref, ylru_ref, ysb_ref, gsb_ref, wout_ref, g_ref, b_ref, o_ref):
    ysb = ysb_ref[...]
    ms = jnp.mean(ysb * ysb, axis=-1, keepdims=True)
    ysb_n = (ysb * lax.rsqrt(ms + RMS_EPS) * gsb_ref[...]).astype(BF16)
    mix = (jnp.dot(ylru_ref[...], wout_ref[0:LRU_WIDTH, :], preferred_element_type=F32)
           + jnp.dot(ysb_n, wout_ref[LRU_WIDTH:, :], preferred_element_type=F32))
    o_ref[...] = _layer_norm(ALPHA * h_ref[...] + mix, g_ref[...], b_ref[...])


def _mix_out(h, ylru, ysb, g_sb, w_out, g, b):
    n = h.shape[0]
    tm = TOKEN_TILE
    return pl.pallas_call(
        _mix_out_kernel,
        grid=(n // tm,),
        in_specs=[
            pl.BlockSpec((tm, D_MODEL), lambda i: (i, 0)),
            pl.BlockSpec((tm, LRU_WIDTH), lambda i: (i, 0)),
            pl.BlockSpec((tm, SB_WIDTH), lambda i: (i, 0)),
            _const_spec((1, SB_WIDTH)),
            _const_spec((D_MODEL, D_MODEL)),
            _const_spec((1, D_MODEL)),
            _const_spec((1, D_MODEL)),
        ],
        out_specs=pl.BlockSpec((tm, D_MODEL), lambda i: (i, 0)),
        out_shape=jax.ShapeDtypeStruct((n, D_MODEL), F32),
        compiler_params=_cparams("parallel"),
        name="mix_out",
    )(h, ylru, ysb, g_sb, w_out, g, b)


def _mem_kv_kernel(mem_ref, wk_t_ref, wv_ref, kt_ref, v_ref):
    mb = mem_ref[0].astype(BF16)
    kt = lax.dot_general(wk_t_ref[...], mb, (((1,), (1,)), ((), ())), preferred_element_type=F32)
    kt_ref[0] = kt.astype(BF16)
    v_ref[0] = jnp.dot(mb, wv_ref[...], preferred_element_type=F32).astype(BF16)


def _mem_kv(mem, wk_t, wv):
    bsz, mlen, _ = mem.shape
    return pl.pallas_call(
        _mem_kv_kernel,
        grid=(bsz,),
        in_specs=[
            pl.BlockSpec((1, mlen, D_MODEL), lambda b: (b, 0, 0)),
            _const_spec((D_MODEL, D_MODEL)),
            _const_spec((D_MODEL, D_MODEL)),
        ],
        out_specs=[
            pl.BlockSpec((1, D_MODEL, mlen), lambda b: (b, 0, 0)),
            pl.BlockSpec((1, mlen, D_MODEL), lambda b: (b, 0, 0)),
        ],
        out_shape=[
            jax.ShapeDtypeStruct((bsz, D_MODEL, mlen), BF16),
            jax.ShapeDtypeStruct((bsz, mlen, D_MODEL), BF16),
        ],
        compiler_params=_cparams("parallel"),
        name="mem_kv",
    )(mem, wk_t, wv)


def _cross_kernel(h_ref, kt_ref, v_ref, wq_ref, wo_ref, g_ref, b_ref, o_ref):
    h = h_ref[...]
    q = jnp.dot(h.astype(BF16), wq_ref[...], preferred_element_type=F32)
    q = (q * (1.0 / math.sqrt(MEM_HEAD_DIM))).astype(BF16)
    heads = []
    for hd in range(MEM_HEADS):
        sl = slice(hd * MEM_HEAD_DIM, (hd + 1) * MEM_HEAD_DIM)
        s = jnp.dot(q[:, sl], kt_ref[0, sl, :], preferred_element_type=F32)
        e = jnp.exp(s - jnp.max(s, axis=-1, keepdims=True))
        p = e * (1.0 / jnp.sum(e, axis=-1, keepdims=True))
        heads.append(jnp.dot(p.astype(BF16), v_ref[0, :, sl], preferred_element_type=F32))
    o = jnp.concatenate(heads, axis=-1).astype(BF16)
    cross = jnp.dot(o, wo_ref[...], preferred_element_type=F32)
    o_ref[...] = _layer_norm(ALPHA * h + cross, g_ref[...], b_ref[...])


def _cross(h, kt, v, wq, wo, g, b, seq):
    n = h.shape[0]
    tm = TOKEN_TILE
    per_b = seq // tm
    mlen = v.shape[1]
    return pl.pallas_call(
        _cross_kernel,
        grid=(n // tm,),
        in_specs=[
            pl.BlockSpec((tm, D_MODEL), lambda i: (i, 0)),
            pl.BlockSpec((1, D_MODEL, mlen), lambda i: (i // per_b, 0, 0)),
            pl.BlockSpec((1, mlen, D_MODEL), lambda i: (i // per_b, 0, 0)),
            _const_spec((D_MODEL, D_MODEL)),
            _const_spec((D_MODEL, D_MODEL)),
            _const_spec((1, D_MODEL)),
            _const_spec((1, D_MODEL)),
        ],
        out_specs=pl.BlockSpec((tm, D_MODEL), lambda i: (i, 0)),
        out_shape=jax.ShapeDtypeStruct((n, D_MODEL), F32),
        compiler_params=_cparams("parallel"),
        name="cross",
    )(h, kt, v, wq, wo, g, b)


def _block_diag(w):
    nh, d, _ = w.shape
    eye = jnp.eye(nh, dtype=w.dtype)
    return (eye[:, None, :, None] * w[:, :, None, :]).reshape(nh * d, nh * d)


def kernel(x, mem, ffn1_w13, ffn1_w2, ln1_g, ln1_b, w_in, conv_w, conv_b, w_rgate, b_rgate, w_igate, b_igate, lru_lambda, g_lru, g_sb, w_out, ln2_g, ln2_b, mem_wq, mem_wkv, mem_wo, ln3_g, ln3_b, ffn2_w13, ffn2_w2, ln4_g, ln4_b):
    bsz, seq, _ = x.shape
    assert seq % TOKEN_TILE == 0 and seq % LRU_CHUNK == 0 and seq % SB_QTILE == 0
    assert TOKEN_TILE % SB_BLOCK == 0 and SB_QTILE % SB_BLOCK == 0
    h = x.reshape(bsz * seq, D_MODEL)
    vec = lambda v: v.reshape(1, -1)
    for l in range(DEPTH):
        h = _ffn_ln(h, ffn1_w13[l].astype(BF16), ffn1_w2[l].astype(BF16), vec(ln1_g[l]), vec(ln1_b[l]))
        wxg = w_in[l][:, :2 * LRU_WIDTH].astype(BF16)
        wqkv_t = w_in[l][:, 2 * LRU_WIDTH:].T.astype(BF16)
        xg, qkv = _mixer_in(h, wxg, wqkv_t, bsz, seq)
        ylru = _rglru(xg, conv_w[l], vec(conv_b[l]),
                      _block_diag(w_rgate[l]).astype(BF16), vec(b_rgate[l]),
                      _block_diag(w_igate[l]).astype(BF16), vec(b_igate[l]),
                      vec(lru_lambda[l]), vec(g_lru[l]), bsz, seq)
        ysb = _sb_attn(qkv, bsz, seq).reshape(bsz * seq, SB_WIDTH)
        h = _mix_out(h, ylru, ysb, vec(g_sb[l]), w_out[l].astype(BF16), vec(ln2_g[l]), vec(ln2_b[l]))
        kt, v = _mem_kv(mem, mem_wkv[l][:, :D_MODEL].T.astype(BF16), mem_wkv[l][:, D_MODEL:].astype(BF16))
        h = _cross(h, kt, v, mem_wq[l].astype(BF16), mem_wo[l].astype(BF16), vec(ln3_g[l]), vec(ln3_b[l]), seq)
        h = _ffn_ln(h, ffn2_w13[l].astype(BF16), ffn2_w2[l].astype(BF16), vec(ln4_g[l]), vec(ln4_b[l]))
    return h.reshape(bsz, seq, D_MODEL)
```

```python
import math

import jax
import jax.numpy as jnp
from jax import lax
from jax.experimental import pallas as pl
from jax.experimental.pallas import tpu as pltpu

F32 = jnp.float32
BF16 = jnp.bfloat16

D_MODEL = 1024
DEPTH = 1
LRU_WIDTH = D_MODEL // 2
LRU_HEADS = 8
LRU_HEAD_DIM = LRU_WIDTH // LRU_HEADS
CONV_WIDTH = 4
LRU_C = 8.0
SB_WIDTH = D_MODEL - LRU_WIDTH
SB_HEADS = 8
SB_HEAD_DIM = SB_WIDTH // SB_HEADS
D_FF = 2688
MEM_HEADS = 4
MEM_HEAD_DIM = D_MODEL // MEM_HEADS
ALPHA = (2 * DEPTH) ** 0.25
LN_EPS = 1e-5
RMS_EPS = 1e-6

LOG2E = 1.4426950408889634
LANES = 128

VMEM_LIMIT_BYTES = 56 * 1024 * 1024

TOKEN_TILE = 512
FF_CHUNKS = (1280, 1408)
SB_BLOCK = 256
SB_QTILE = 512
SB_DEAD_LOG2 = 160.0
LRU_CHUNK = 512


def _cparams(*sem):
    return pltpu.CompilerParams(dimension_semantics=sem, vmem_limit_bytes=VMEM_LIMIT_BYTES)


def _const_spec(shape):
    nd = len(shape)
    return pl.BlockSpec(shape, lambda *_: (0,) * nd, pipeline_mode=pl.Buffered(1))


def _layer_norm(r, g, b):
    mu = jnp.mean(r, axis=-1, keepdims=True)
    d = r - mu
    var = jnp.mean(d * d, axis=-1, keepdims=True)
    return d * lax.rsqrt(var + LN_EPS) * g + b


def _sigmoid(x):
    return 0.5 * jnp.tanh(0.5 * x) + 0.5


def _ffn_ln_kernel(x_ref, w13_ref, w2_ref, g_ref, b_ref, o_ref):
    x = x_ref[...]
    xb = x.astype(BF16)
    y = None
    off = 0
    for width in FF_CHUNKS:
        gate = jnp.dot(xb, w13_ref[:, off:off + width], preferred_element_type=F32)
        up = jnp.dot(xb, w13_ref[:, D_FF + off:D_FF + off + width], preferred_element_type=F32)
        act = (gate * _sigmoid(gate) * up).astype(BF16)
        part = jnp.dot(act, w2_ref[off:off + width, :], preferred_element_type=F32)
        y = part if y is None else y + part
        off += width
    o_ref[...] = _layer_norm(ALPHA * x + 0.5 * y, g_ref[...], b_ref[...])


def _ffn_ln(x, w13, w2, g, b):
    n = x.shape[0]
    tm = TOKEN_TILE
    return pl.pallas_call(
        _ffn_ln_kernel,
        grid=(n // tm,),
        in_specs=[
            pl.BlockSpec((tm, D_MODEL), lambda i: (i, 0)),
            _const_spec((D_MODEL, 2 * D_FF)),
            _const_spec((D_FF, D_MODEL)),
            _const_spec((1, D_MODEL)),
            _const_spec((1, D_MODEL)),
        ],
        out_specs=pl.BlockSpec((tm, D_MODEL), lambda i: (i, 0)),
        out_shape=jax.ShapeDtypeStruct((n, D_MODEL), F32),
        compiler_params=_cparams("parallel"),
        name="ffn_ln",
    )(x, w13, w2, g, b)


def _mixer_in_kernel(h_ref, wxgq_ref, wkv_t_ref, xg_ref, q_ref, kv_ref):
    hb = h_ref[...].astype(BF16)
    xgq = jnp.dot(hb, wxgq_ref[...], preferred_element_type=F32)
    xg_ref[...] = xgq[:, :2 * LRU_WIDTH]
    q_ref[...] = (xgq[:, 2 * LRU_WIDTH:] * (LOG2E / math.sqrt(SB_HEAD_DIM))).astype(BF16)
    t = lax.dot_general(wkv_t_ref[...], hb, (((1,), (1,)), ((), ())), preferred_element_type=F32)
    tm = hb.shape[0]
    for c in range(tm // SB_BLOCK):
        blk = t[:, c * SB_BLOCK:(c + 1) * SB_BLOCK]
        for j in range(2 * SB_HEADS):
            kv_ref[0, j, c] = blk[j * SB_HEAD_DIM:(j + 1) * SB_HEAD_DIM, :].astype(BF16)


def _mixer_in(h, wxgq, wkv_t, bsz, seq):
    n = h.shape[0]
    tm = TOKEN_TILE
    per_b = seq // tm
    nbt = tm // SB_BLOCK
    nh = 2 * SB_HEADS
    return pl.pallas_call(
        _mixer_in_kernel,
        grid=(n // tm,),
        in_specs=[
            pl.BlockSpec((tm, D_MODEL), lambda i: (i, 0)),
            _const_spec((D_MODEL, 2 * LRU_WIDTH + SB_WIDTH)),
            _const_spec((2 * SB_WIDTH, D_MODEL)),
        ],
        out_specs=[
            pl.BlockSpec((tm, 2 * LRU_WIDTH), lambda i: (i, 0)),
            pl.BlockSpec((tm, SB_WIDTH), lambda i: (i, 0)),
            pl.BlockSpec((1, nh, nbt, SB_HEAD_DIM, SB_BLOCK),
                         lambda i: (i // per_b, 0, i % per_b, 0, 0)),
        ],
        out_shape=[
            jax.ShapeDtypeStruct((n, 2 * LRU_WIDTH), F32),
            jax.ShapeDtypeStruct((n, SB_WIDTH), BF16),
            jax.ShapeDtypeStruct((bsz, nh, seq // SB_BLOCK, SB_HEAD_DIM, SB_BLOCK), BF16),
        ],
        compiler_params=_cparams("parallel"),
        name="mixer_in",
    )(h, wxgq, wkv_t)


def _rglru_kernel(xg_ref, cw_ref, cb_ref, wr_ref, br_ref, wi_ref, bi_ref, lam_ref, g_ref,
                  o_ref, xpad_ref, a_ref, u_ref, hs_ref, hcar_ref):
    ts = LRU_CHUNK
    c = pl.program_id(1)

    @pl.when(c == 0)
    def _():
        xpad_ref[0:8, :] = jnp.zeros((8, LRU_WIDTH), F32)
        hcar_ref[...] = jnp.zeros_like(hcar_ref)

    x = xg_ref[:, 0:LRU_WIDTH]
    gate = xg_ref[:, LRU_WIDTH:2 * LRU_WIDTH]
    xpad_ref[8:8 + ts, :] = x
    xc = cb_ref[...] + x * cw_ref[CONV_WIDTH - 1:CONV_WIDTH, :]
    for k in range(CONV_WIDTH - 1):
        back = CONV_WIDTH - 1 - k
        xc = xc + xpad_ref[8 - back:8 - back + ts, :] * cw_ref[k:k + 1, :]
    xpad_ref[0:8, :] = x[ts - 8:ts, :]

    xcb = xc.astype(BF16)
    r = _sigmoid(jnp.dot(xcb, wr_ref[...], preferred_element_type=F32) + br_ref[...])
    i = _sigmoid(jnp.dot(xcb, wi_ref[...], preferred_element_type=F32) + bi_ref[...])
    nlam = -lam_ref[...]
    softplus = jnp.maximum(nlam, 0.0) + jnp.log1p(jnp.exp(-jnp.abs(nlam)))
    log_a = (-LRU_C) * r * softplus
    a_ref[...] = jnp.exp(log_a)
    th = jnp.tanh(log_a)
    u_ref[...] = jnp.sqrt(-2.0 * th / (1.0 - th)) * (i * xc)

    def step(t, h):
        h = a_ref[pl.ds(t, 1), :] * h + u_ref[pl.ds(t, 1), :]
        hs_ref[pl.ds(t, 1), :] = h
        return h

    hcar_ref[...] = lax.fori_loop(0, ts, step, hcar_ref[...], unroll=8)

    gelu = 0.5 * gate * (1.0 + jnp.tanh(math.sqrt(2.0 / math.pi) * (gate + 0.044715 * (gate * gate * gate))))
    y = hs_ref[...] * gelu
    ms = jnp.mean(y * y, axis=-1, keepdims=True)
    o_ref[...] = (y * lax.rsqrt(ms + RMS_EPS) * g_ref[...]).astype(BF16)


def _rglru(xg, conv_w, conv_b, wr, br, wi, bi, lam, g_lru, bsz, seq):
    n = xg.shape[0]
    ts = LRU_CHUNK
    per_b = seq // ts
    row = lambda b, c: (b * per_b + c, 0)
    vec = _const_spec((1, LRU_WIDTH))
    return pl.pallas_call(
        _rglru_kernel,
        grid=(bsz, per_b),
        in_specs=[
            pl.BlockSpec((ts, 2 * LRU_WIDTH), row),
            _const_spec((CONV_WIDTH, LRU_WIDTH)), vec,
            _const_spec((LRU_WIDTH, LRU_WIDTH)), vec,
            _const_spec((LRU_WIDTH, LRU_WIDTH)), vec,
            vec, vec,
        ],
        out_specs=pl.BlockSpec((ts, LRU_WIDTH), row),
        out_shape=jax.ShapeDtypeStruct((n, LRU_WIDTH), BF16),
        scratch_shapes=[
            pltpu.VMEM((ts + 8, LRU_WIDTH), F32),
            pltpu.VMEM((ts, LRU_WIDTH), F32),
            pltpu.VMEM((ts, LRU_WIDTH), F32),
            pltpu.VMEM((ts, LRU_WIDTH), F32),
            pltpu.VMEM((1, LRU_WIDTH), F32),
        ],
        compiler_params=_cparams("parallel", "arbitrary"),
        name="rglru",
    )(xg, conv_w, conv_b, wr, br, wi, bi, lam, g_lru)


def _sb_head_block(q, kt, vt_pad, tri, carry, causal):
    z = jnp.dot(q, kt, preferred_element_type=F32)
    neg_abs = lax.bitcast_convert_type(lax.bitcast_convert_type(z, jnp.int32) | jnp.int32(-2 ** 31), F32)
    p = jnp.maximum(z, 0.0) + jnp.log2(1.0 + jnp.exp2(neg_abs))
    if causal is not None:
        p = jnp.where(causal, p, 0.0)
    c = jnp.dot(p.astype(BF16), tri, preferred_element_type=F32)
    d = jnp.minimum(z - c, 0.0)
    w = jnp.exp2(d - jnp.concatenate([carry] * (z.shape[1] // LANES), axis=1))
    if causal is not None:
        w = jnp.where(causal, w, 0.0)
    pv = lax.dot_general(w.astype(BF16), vt_pad, (((1,), (1,)), ((), ())), preferred_element_type=F32)
    return carry + jnp.broadcast_to(c[:, 0:1], carry.shape), pv


def _sb_attn_kernel(q_ref, k_ref, v_ref, o_ref, acc_ref):
    tb, tq = SB_BLOCK, SB_QTILE
    r = tq // tb
    qi = pl.program_id(2)
    tri_row = lax.broadcasted_iota(jnp.int32, (tb, tb), 0)
    tri_col = lax.broadcasted_iota(jnp.int32, (tb, tb), 1)
    tri = (tri_row >= tri_col).astype(BF16)
    zpad = jnp.zeros((SB_HEAD_DIM, tb), BF16)
    q = q_ref[...]

    def step(kb, carries, lo, causal, keep=None):
        new, total = [], None
        for hh in range(2):
            pad = (lambda x: jnp.concatenate([x, zpad], axis=0)) if hh == 0 else (
                lambda x: jnp.concatenate([zpad, x], axis=0))
            cnew, pv = _sb_head_block(q[lo:], pad(k_ref[0, hh, kb]), pad(v_ref[0, hh, kb]), tri,
                                      carries[hh][lo:], causal)
            new.append(cnew if lo == 0 else jnp.concatenate([carries[hh][:lo], cnew], axis=0))
            total = pv if total is None else total + pv
        if keep is not None:
            total = jnp.where(keep, total, 0.0)
        acc_ref[lo:, :] += total
        return tuple(new)

    acc_ref[...] = jnp.zeros_like(acc_ref)
    carries = (jnp.zeros((tq, LANES), F32), jnp.zeros((tq, LANES), F32))
    for m in reversed(range(r)):
        lo = m * tb
        row = lax.broadcasted_iota(jnp.int32, (tq - lo, tb), 0)
        col = lax.broadcasted_iota(jnp.int32, (tq - lo, tb), 1)
        carries = step(r * qi + m, carries, lo, col < row)

    has_left = qi > 0
    for m in reversed(range(r)):
        carries = step(r * jnp.maximum(qi - 1, 0) + m, carries, 0, None, keep=has_left)

    def body(state):
        j, carries = state[0], state[1:]
        for m in reversed(range(r)):
            carries = step(r * (qi - 1 - j) + m, carries, 0, None)
        return (j + 1,) + carries

    def more(state):
        j, c0, c1 = state
        return jnp.logical_and(j < qi, jnp.min(jnp.minimum(c0, c1)) < SB_DEAD_LOG2)

    lax.while_loop(more, body, (jnp.int32(1),) + carries)
    o_ref[0] = acc_ref[...]


def _sb_attn(q, kv, bsz, seq):
    tb, tq = SB_BLOCK, SB_QTILE
    nb = seq // tb
    hp = SB_HEADS // 2
    kv_block = (1, 2, nb, SB_HEAD_DIM, tb)
    return pl.pallas_call(
        _sb_attn_kernel,
        grid=(bsz, hp, seq // tq),
        in_specs=[
            pl.BlockSpec((tq, 2 * SB_HEAD_DIM), lambda b, h, i: (b * (seq // tq) + i, h)),
            pl.BlockSpec(kv_block, lambda b, h, i: (b, h, 0, 0, 0)),
            pl.BlockSpec(kv_block, lambda b, h, i: (b, hp + h, 0, 0, 0)),
        ],
        out_specs=pl.BlockSpec((1, tq, 2 * SB_HEAD_DIM), lambda b, h, i: (b, i, h)),
        out_shape=jax.ShapeDtypeStruct((bsz, seq, SB_WIDTH), F32),
        scratch_shapes=[pltpu.VMEM((tq, 2 * SB_HEAD_DIM), F32)],
        compiler_params=_cparams("parallel", "parallel", "arbitrary"),
        name="sb_attn",
    )(q, kv, kv)


def _mem_kv_kernel(mem_ref, wk_t_ref, wv_ref, kt_ref, v_ref):
    mb = mem_ref[0].astype(BF16)
    kt = lax.dot_general(wk_t_ref[...], mb, (((1,), (1,)), ((), ())), preferred_element_type=F32)
    kt_ref[0] = kt.astype(BF16)
    v_ref[0] = jnp.dot(mb, wv_ref[...], preferred_element_type=F32).astype(BF16)


def _mem_kv(mem, wk_t, wv):
    bsz, mlen, _ = mem.shape
    return pl.pallas_call(
        _mem_kv_kernel,
        grid=(bsz,),
        in_specs=[
            pl.BlockSpec((1, mlen, D_MODEL), lambda b: (b, 0, 0)),
            _const_spec((D_MODEL, D_MODEL)),
            _const_spec((D_MODEL, D_MODEL)),
        ],
        out_specs=[
            pl.BlockSpec((1, D_MODEL, mlen), lambda b: (b, 0, 0)),
            pl.BlockSpec((1, mlen, D_MODEL), lambda b: (b, 0, 0)),
        ],
        out_shape=[
            jax.ShapeDtypeStruct((bsz, D_MODEL, mlen), BF16),
            jax.ShapeDtypeStruct((bsz, mlen, D_MODEL), BF16),
        ],
        compiler_params=_cparams("parallel"),
        name="mem_kv",
    )(mem, wk_t, wv)


def _mix_cross_kernel(h_ref, ylru_ref, ysb_ref, gsb_ref, wout_ref, g2_ref, b2_ref,
                      kt_ref, v_ref, wq_ref, wo_ref, g3_ref, b3_ref, o_ref):
    ysb = ysb_ref[...]
    ms = jnp.mean(ysb * ysb, axis=-1, keepdims=True)
    ysb_n = (ysb * lax.rsqrt(ms + RMS_EPS) * gsb_ref[...]).astype(BF16)
    mix = (jnp.dot(ylru_ref[...], wout_ref[0:LRU_WIDTH, :], preferred_element_type=F32)
           + jnp.dot(ysb_n, wout_ref[LRU_WIDTH:, :], preferred_element_type=F32))
    h = _layer_norm(ALPHA * h_ref[...] + mix, g2_ref[...], b2_ref[...])
    q = jnp.dot(h.astype(BF16), wq_ref[...], preferred_element_type=F32)
    q = (q * (1.0 / math.sqrt(MEM_HEAD_DIM))).astype(BF16)
    heads = []
    for hd in range(MEM_HEADS):
        sl = slice(hd * MEM_HEAD_DIM, (hd + 1) * MEM_HEAD_DIM)
        s = jnp.dot(q[:, sl], kt_ref[0, sl, :], preferred_element_type=F32)
        e = jnp.exp(s - jnp.max(s, axis=-1, keepdims=True))
        p = e * (1.0 / jnp.sum(e, axis=-1, keepdims=True))
        heads.append(jnp.dot(p.astype(BF16), v_ref[0, :, sl], preferred_element_type=F32))
    o = jnp.concatenate(heads, axis=-1).astype(BF16)
    cross = jnp.dot(o, wo_ref[...], preferred_element_type=F32)
    o_ref[...] = _layer_norm(ALPHA * h + cross, g3_ref[...], b3_ref[...])


def _mix_cross(h, ylru, ysb, g_sb, w_out, g2, b2, kt, v, wq, wo, g3, b3, seq):
    n = h.shape[0]
    tm = 2 * TOKEN_TILE
    per_b = seq // tm
    mlen = v.shape[1]
    vec = _const_spec((1, D_MODEL))
    mat = _const_spec((D_MODEL, D_MODEL))
    return pl.pallas_call(
        _mix_cross_kernel,
        grid=(n // tm,),
        in_specs=[
            pl.BlockSpec((tm, D_MODEL), lambda i: (i, 0)),
            pl.BlockSpec((tm, LRU_WIDTH), lambda i: (i, 0)),
            pl.BlockSpec((tm, SB_WIDTH), lambda i: (i, 0)),
            _const_spec((1, SB_WIDTH)),
            mat, vec, vec,
            pl.BlockSpec((1, D_MODEL, mlen), lambda i: (i // per_b, 0, 0)),
            pl.BlockSpec((1, mlen, D_MODEL), lambda i: (i // per_b, 0, 0)),
            mat, mat, vec, vec,
        ],
        out_specs=pl.BlockSpec((tm, D_MODEL), lambda i: (i, 0)),
        out_shape=jax.ShapeDtypeStruct((n, D_MODEL), F32),
        compiler_params=_cparams("parallel"),
        name="mix_cross",
    )(h, ylru, ysb, g_sb, w_out, g2, b2, kt, v, wq, wo, g3, b3)


def _block_diag(w):
    nh, d, _ = w.shape
    eye = jnp.eye(nh, dtype=w.dtype)
    return (eye[:, None, :, None] * w[:, :, None, :]).reshape(nh * d, nh * d)


def kernel(x, mem, ffn1_w13, ffn1_w2, ln1_g, ln1_b, w_in, conv_w, conv_b, w_rgate, b_rgate, w_igate, b_igate, lru_lambda, g_lru, g_sb, w_out, ln2_g, ln2_b, mem_wq, mem_wkv, mem_wo, ln3_g, ln3_b, ffn2_w13, ffn2_w2, ln4_g, ln4_b):
    bsz, seq, _ = x.shape
    assert seq % TOKEN_TILE == 0 and seq % LRU_CHUNK == 0 and seq % SB_QTILE == 0
    assert TOKEN_TILE % SB_BLOCK == 0 and SB_QTILE % SB_BLOCK == 0
    h = x.reshape(bsz * seq, D_MODEL)
    vec = lambda v: v.reshape(1, -1)
    for l in range(DEPTH):
        h = _ffn_ln(h, ffn1_w13[l].astype(BF16), ffn1_w2[l].astype(BF16), vec(ln1_g[l]), vec(ln1_b[l]))
        wxgq = w_in[l][:, :2 * LRU_WIDTH + SB_WIDTH].astype(BF16)
        wkv_t = w_in[l][:, 2 * LRU_WIDTH + SB_WIDTH:].T.astype(BF16)
        xg, q, kv = _mixer_in(h, wxgq, wkv_t, bsz, seq)
        ylru = _rglru(xg, conv_w[l], vec(conv_b[l]),
                      _block_diag(w_rgate[l]).astype(BF16), vec(b_rgate[l]),
                      _block_diag(w_igate[l]).astype(BF16), vec(b_igate[l]),
                      vec(lru_lambda[l]), vec(g_lru[l]), bsz, seq)
        ysb = _sb_attn(q, kv, bsz, seq).reshape(bsz * seq, SB_WIDTH)
        kt, v = _mem_kv(mem, mem_wkv[l][:, :D_MODEL].T.astype(BF16), mem_wkv[l][:, D_MODEL:].astype(BF16))
        h = _mix_cross(h, ylru, ysb, vec(g_sb[l]), w_out[l].astype(BF16), vec(ln2_g[l]), vec(ln2_b[l]),
                       kt, v, mem_wq[l].astype(BF16), mem_wo[l].astype(BF16), vec(ln3_g[l]), vec(ln3_b[l]), seq)
        h = _ffn_ln(h, ffn2_w13[l].astype(BF16), ffn2_w2[l].astype(BF16), vec(ln4_g[l]), vec(ln4_b[l]))
    return h.reshape(bsz, seq, D_MODEL)
```

```python
import math

import jax
import jax.numpy as jnp
from jax import lax
from jax.experimental import pallas as pl
from jax.experimental.pallas import tpu as pltpu

F32 = jnp.float32
BF16 = jnp.bfloat16

D_MODEL = 1024
DEPTH = 1
LRU_WIDTH = D_MODEL // 2
LRU_HEADS = 8
LRU_HEAD_DIM = LRU_WIDTH // LRU_HEADS
CONV_WIDTH = 4
LRU_C = 8.0
SB_WIDTH = D_MODEL - LRU_WIDTH
SB_HEADS = 8
SB_HEAD_DIM = SB_WIDTH // SB_HEADS
D_FF = 2688
MEM_HEADS = 4
MEM_HEAD_DIM = D_MODEL // MEM_HEADS
ALPHA = (2 * DEPTH) ** 0.25
LN_EPS = 1e-5
RMS_EPS = 1e-6

LOG2E = 1.4426950408889634
LANES = 128

VMEM_LIMIT_BYTES = 56 * 1024 * 1024

TOKEN_TILE = 512
FF_CHUNKS = (1280, 1408)
SB_BLOCK = 256
SB_QTILE = 512
SB_DEAD_LOG2 = 160.0
LRU_CHUNK = 512


def _cparams(*sem):
    return pltpu.CompilerParams(dimension_semantics=sem, vmem_limit_bytes=VMEM_LIMIT_BYTES)


def _const_spec(shape):
    nd = len(shape)
    return pl.BlockSpec(shape, lambda *_: (0,) * nd, pipeline_mode=pl.Buffered(1))


def _layer_norm(r, g, b):
    mu = jnp.mean(r, axis=-1, keepdims=True)
    d = r - mu
    var = jnp.mean(d * d, axis=-1, keepdims=True)
    return d * lax.rsqrt(var + LN_EPS) * g + b


def _sigmoid(x):
    return 0.5 * jnp.tanh(0.5 * x) + 0.5


def _ffn_ln_kernel(x_ref, w13_ref, w2_ref, g_ref, b_ref, o_ref):
    x = x_ref[...]
    xb = x.astype(BF16)
    y = None
    off = 0
    for width in FF_CHUNKS:
        gu = jnp.dot(xb, w13_ref[:, 2 * off:2 * (off + width)], preferred_element_type=F32)
        gate, up = gu[:, :width], gu[:, width:]
        act = (gate * _sigmoid(gate) * up).astype(BF16)
        part = jnp.dot(act, w2_ref[off:off + width, :], preferred_element_type=F32)
        y = part if y is None else y + part
        off += width
    o_ref[...] = _layer_norm(ALPHA * x + 0.5 * y, g_ref[...], b_ref[...])


def _ffn_w13(w13):
    parts, off = [], 0
    for width in FF_CHUNKS:
        parts += [w13[:, off:off + width], w13[:, D_FF + off:D_FF + off + width]]
        off += width
    return jnp.concatenate(parts, axis=1).astype(BF16)


def _ffn_ln(x, w13, w2, g, b):
    n = x.shape[0]
    tm = TOKEN_TILE
    return pl.pallas_call(
        _ffn_ln_kernel,
        grid=(n // tm,),
        in_specs=[
            pl.BlockSpec((tm, D_MODEL), lambda i: (i, 0)),
            _const_spec((D_MODEL, 2 * D_FF)),
            _const_spec((D_FF, D_MODEL)),
            _const_spec((1, D_MODEL)),
            _const_spec((1, D_MODEL)),
        ],
        out_specs=pl.BlockSpec((tm, D_MODEL), lambda i: (i, 0)),
        out_shape=jax.ShapeDtypeStruct((n, D_MODEL), F32),
        compiler_params=_cparams("parallel"),
        name="ffn_ln",
    )(x, w13, w2, g, b)


def _mixer_in_kernel(h_ref, wxgq_ref, wkv_t_ref, xg_ref, q_ref, kv_ref):
    hb = h_ref[...].astype(BF16)
    xgq = jnp.dot(hb, wxgq_ref[...], preferred_element_type=F32)
    xg_ref[...] = xgq[:, :2 * LRU_WIDTH]
    q_ref[...] = (xgq[:, 2 * LRU_WIDTH:] * (LOG2E / math.sqrt(SB_HEAD_DIM))).astype(BF16)
    t = lax.dot_general(wkv_t_ref[...], hb, (((1,), (1,)), ((), ())), preferred_element_type=F32)
    tm = hb.shape[0]
    for c in range(tm // SB_BLOCK):
        blk = t[:, c * SB_BLOCK:(c + 1) * SB_BLOCK]
        for j in range(2 * SB_HEADS):
            kv_ref[0, j, c] = blk[j * SB_HEAD_DIM:(j + 1) * SB_HEAD_DIM, :].astype(BF16)


def _mixer_in(h, wxgq, wkv_t, bsz, seq):
    n = h.shape[0]
    tm = TOKEN_TILE
    per_b = seq // tm
    nbt = tm // SB_BLOCK
    nh = 2 * SB_HEADS
    return pl.pallas_call(
        _mixer_in_kernel,
        grid=(n // tm,),
        in_specs=[
            pl.BlockSpec((tm, D_MODEL), lambda i: (i, 0)),
            _const_spec((D_MODEL, 2 * LRU_WIDTH + SB_WIDTH)),
            _const_spec((2 * SB_WIDTH, D_MODEL)),
        ],
        out_specs=[
            pl.BlockSpec((tm, 2 * LRU_WIDTH), lambda i: (i, 0)),
            pl.BlockSpec((tm, SB_WIDTH), lambda i: (i, 0)),
            pl.BlockSpec((1, nh, nbt, SB_HEAD_DIM, SB_BLOCK),
                         lambda i: (i // per_b, 0, i % per_b, 0, 0)),
        ],
        out_shape=[
            jax.ShapeDtypeStruct((n, 2 * LRU_WIDTH), F32),
            jax.ShapeDtypeStruct((n, SB_WIDTH), BF16),
            jax.ShapeDtypeStruct((bsz, nh, seq // SB_BLOCK, SB_HEAD_DIM, SB_BLOCK), BF16),
        ],
        compiler_params=_cparams("parallel"),
        name="mixer_in",
    )(h, wxgq, wkv_t)


def _rglru_kernel(xg_ref, cw_ref, cb_ref, wr_ref, br_ref, wi_ref, bi_ref, lam_ref, g_ref,
                  o_ref, xpad_ref, a_ref, u_ref, hs_ref, hcar_ref):
    ts = LRU_CHUNK
    c = pl.program_id(1)

    @pl.when(c == 0)
    def _():
        xpad_ref[0:8, :] = jnp.zeros((8, LRU_WIDTH), F32)
        hcar_ref[...] = jnp.zeros_like(hcar_ref)

    x = xg_ref[:, 0:LRU_WIDTH]
    gate = xg_ref[:, LRU_WIDTH:2 * LRU_WIDTH]
    xpad_ref[8:8 + ts, :] = x
    xc = cb_ref[...] + x * cw_ref[CONV_WIDTH - 1:CONV_WIDTH, :]
    for k in range(CONV_WIDTH - 1):
        back = CONV_WIDTH - 1 - k
        xc = xc + xpad_ref[8 - back:8 - back + ts, :] * cw_ref[k:k + 1, :]
    xpad_ref[0:8, :] = x[ts - 8:ts, :]

    xcb = xc.astype(BF16)
    r = _sigmoid(jnp.dot(xcb, wr_ref[...], preferred_element_type=F32) + br_ref[...])
    i = _sigmoid(jnp.dot(xcb, wi_ref[...], preferred_element_type=F32) + bi_ref[...])
    nlam = -lam_ref[...]
    softplus = jnp.maximum(nlam, 0.0) + jnp.log1p(jnp.exp(-jnp.abs(nlam)))
    log_a = (-LRU_C) * r * softplus
    a_ref[...] = jnp.exp(log_a)
    th = jnp.tanh(log_a)
    n = -2.0 * th
    root = jnp.where(n > 0.0, n * lax.rsqrt(n * (1.0 - th)), 0.0)
    u_ref[...] = root * (i * xc)

    def step(t, h):
        h = a_ref[pl.ds(t, 1), :] * h + u_ref[pl.ds(t, 1), :]
        hs_ref[pl.ds(t, 1), :] = h
        return h

    hcar_ref[...] = lax.fori_loop(0, ts, step, hcar_ref[...], unroll=8)

    gelu = 0.5 * gate * (1.0 + jnp.tanh(math.sqrt(2.0 / math.pi) * (gate + 0.044715 * (gate * gate * gate))))
    y = hs_ref[...] * gelu
    ms = jnp.mean(y * y, axis=-1, keepdims=True)
    o_ref[...] = (y * lax.rsqrt(ms + RMS_EPS) * g_ref[...]).astype(BF16)


def _rglru(xg, conv_w, conv_b, wr, br, wi, bi, lam, g_lru, bsz, seq):
    n = xg.shape[0]
    ts = LRU_CHUNK
    per_b = seq // ts
    row = lambda b, c: (b * per_b + c, 0)
    vec = _const_spec((1, LRU_WIDTH))
    return pl.pallas_call(
        _rglru_kernel,
        grid=(bsz, per_b),
        in_specs=[
            pl.BlockSpec((ts, 2 * LRU_WIDTH), row),
            _const_spec((CONV_WIDTH, LRU_WIDTH)), vec,
            _const_spec((LRU_WIDTH, LRU_WIDTH)), vec,
            _const_spec((LRU_WIDTH, LRU_WIDTH)), vec,
            vec, vec,
        ],
        out_specs=pl.BlockSpec((ts, LRU_WIDTH), row),
        out_shape=jax.ShapeDtypeStruct((n, LRU_WIDTH), BF16),
        scratch_shapes=[
            pltpu.VMEM((ts + 8, LRU_WIDTH), F32),
            pltpu.VMEM((ts, LRU_WIDTH), F32),
            pltpu.VMEM((ts, LRU_WIDTH), F32),
            pltpu.VMEM((ts, LRU_WIDTH), F32),
            pltpu.VMEM((1, LRU_WIDTH), F32),
        ],
        compiler_params=_cparams("parallel", "arbitrary"),
        name="rglru",
    )(xg, conv_w, conv_b, wr, br, wi, bi, lam, g_lru)


def _sb_head_block(q, kt, vt_pad, tri, carry, causal):
    z = jnp.dot(q, kt, preferred_element_type=F32)
    neg_abs = lax.bitcast_convert_type(lax.bitcast_convert_type(z, jnp.int32) | jnp.int32(-2 ** 31), F32)
    p = jnp.maximum(z, 0.0) + jnp.log2(1.0 + jnp.exp2(neg_abs))
    if causal is not None:
        p = jnp.where(causal, p, 0.0)
    c = jnp.dot(p.astype(BF16), tri, preferred_element_type=F32)
    d = jnp.minimum(z - c, 0.0)
    w = jnp.exp2(d - jnp.concatenate([carry] * (z.shape[1] // LANES), axis=1))
    if causal is not None:
        w = jnp.where(causal, w, 0.0)
    pv = lax.dot_general(w.astype(BF16), vt_pad, (((1,), (1,)), ((), ())), preferred_element_type=F32)
    return carry + jnp.broadcast_to(c[:, 0:1], carry.shape), pv


def _sb_attn_kernel(q_ref, k_ref, v_ref, o_ref, acc_ref):
    tb, tq = SB_BLOCK, SB_QTILE
    r = tq // tb
    qi = pl.program_id(2)
    tri_row = lax.broadcasted_iota(jnp.int32, (tb, tb), 0)
    tri_col = lax.broadcasted_iota(jnp.int32, (tb, tb), 1)
    tri = (tri_row >= tri_col).astype(BF16)
    zpad = jnp.zeros((SB_HEAD_DIM, tb), BF16)
    q = q_ref[...]

    def step(kb, carries, lo, causal, keep=None):
        new, total = [], None
        for hh in range(2):
            pad = (lambda x: jnp.concatenate([x, zpad], axis=0)) if hh == 0 else (
                lambda x: jnp.concatenate([zpad, x], axis=0))
            cnew, pv = _sb_head_block(q[lo:], pad(k_ref[0, hh, kb]), pad(v_ref[0, hh, kb]), tri,
                                      carries[hh][lo:], causal)
            new.append(cnew if lo == 0 else jnp.concatenate([carries[hh][:lo], cnew], axis=0))
            total = pv if total is None else total + pv
        if keep is not None:
            total = jnp.where(keep, total, 0.0)
        acc_ref[lo:, :] += total
        return tuple(new)

    acc_ref[...] = jnp.zeros_like(acc_ref)
    carries = (jnp.zeros((tq, LANES), F32), jnp.zeros((tq, LANES), F32))
    for m in reversed(range(r)):
        lo = m * tb
        row = lax.broadcasted_iota(jnp.int32, (tq - lo, tb), 0)
        col = lax.broadcasted_iota(jnp.int32, (tq - lo, tb), 1)
        carries = step(r * qi + m, carries, lo, col < row)

    n_left = r * qi
    carries = step(jnp.maximum(n_left - 1, 0), carries, 0, None, keep=n_left > 0)

    def body(state):
        j, carries = state[0], state[1:]
        return (j + 1,) + step(n_left - 1 - j, carries, 0, None)

    def more(state):
        j, c0, c1 = state
        return jnp.logical_and(j < n_left, jnp.min(jnp.minimum(c0, c1)) < SB_DEAD_LOG2)

    lax.while_loop(more, body, (jnp.int32(1),) + carries)
    o_ref[0] = acc_ref[...]


def _sb_attn(q, kv, bsz, seq):
    tb, tq = SB_BLOCK, SB_QTILE
    nb = seq // tb
    hp = SB_HEADS // 2
    kv_block = (1, 2, nb, SB_HEAD_DIM, tb)
    return pl.pallas_call(
        _sb_attn_kernel,
        grid=(bsz, hp, seq // tq),
        in_specs=[
            pl.BlockSpec((tq, 2 * SB_HEAD_DIM), lambda b, h, i: (b * (seq // tq) + i, h)),
            pl.BlockSpec(kv_block, lambda b, h, i: (b, h, 0, 0, 0)),
            pl.BlockSpec(kv_block, lambda b, h, i: (b, hp + h, 0, 0, 0)),
        ],
        out_specs=pl.BlockSpec((1, tq, 2 * SB_HEAD_DIM), lambda b, h, i: (b, i, h)),
        out_shape=jax.ShapeDtypeStruct((bsz, seq, SB_WIDTH), F32),
        scratch_shapes=[pltpu.VMEM((tq, 2 * SB_HEAD_DIM), F32)],
        compiler_params=_cparams("parallel", "parallel", "arbitrary"),
        name="sb_attn",
    )(q, kv, kv)


def _mem_kv_kernel(mem_ref, wk_t_ref, wv_ref, kt_ref, v_ref):
    mb = mem_ref[0].astype(BF16)
    kt = lax.dot_general(wk_t_ref[...], mb, (((1,), (1,)), ((), ())), preferred_element_type=F32)
    kt_ref[0] = kt.astype(BF16)
    v_ref[0] = jnp.dot(mb, wv_ref[...], preferred_element_type=F32).astype(BF16)


def _mem_kv(mem, wk_t, wv):
    bsz, mlen, _ = mem.shape
    return pl.pallas_call(
        _mem_kv_kernel,
        grid=(bsz,),
        in_specs=[
            pl.BlockSpec((1, mlen, D_MODEL), lambda b: (b, 0, 0)),
            _const_spec((D_MODEL, D_MODEL)),
            _const_spec((D_MODEL, D_MODEL)),
        ],
        out_specs=[
            pl.BlockSpec((1, D_MODEL, mlen), lambda b: (b, 0, 0)),
            pl.BlockSpec((1, mlen, D_MODEL), lambda b: (b, 0, 0)),
        ],
        out_shape=[
            jax.ShapeDtypeStruct((bsz, D_MODEL, mlen), BF16),
            jax.ShapeDtypeStruct((bsz, mlen, D_MODEL), BF16),
        ],
        compiler_params=_cparams("parallel"),
        name="mem_kv",
    )(mem, wk_t, wv)


def _mix_cross_kernel(h_ref, ylru_ref, ysb_ref, gsb_ref, wout_ref, g2_ref, b2_ref,
                      kt_ref, v_ref, wq_ref, wo_ref, g3_ref, b3_ref, o_ref):
    ysb = ysb_ref[...]
    ms = jnp.mean(ysb * ysb, axis=-1, keepdims=True)
    ysb_n = (ysb * lax.rsqrt(ms + RMS_EPS) * gsb_ref[...]).astype(BF16)
    mix = (jnp.dot(ylru_ref[...], wout_ref[0:LRU_WIDTH, :], preferred_element_type=F32)
           + jnp.dot(ysb_n, wout_ref[LRU_WIDTH:, :], preferred_element_type=F32))
    h = _layer_norm(ALPHA * h_ref[...] + mix, g2_ref[...], b2_ref[...])
    q = jnp.dot(h.astype(BF16), wq_ref[...], preferred_element_type=F32)
    q = (q * (1.0 / math.sqrt(MEM_HEAD_DIM))).astype(BF16)
    heads = []
    for hd in range(MEM_HEADS):
        sl = slice(hd * MEM_HEAD_DIM, (hd + 1) * MEM_HEAD_DIM)
        s = jnp.dot(q[:, sl], kt_ref[0, sl, :], preferred_element_type=F32)
        e = jnp.exp(s - jnp.max(s, axis=-1, keepdims=True))
        p = e * (1.0 / jnp.sum(e, axis=-1, keepdims=True))
        heads.append(jnp.dot(p.astype(BF16), v_ref[0, :, sl], preferred_element_type=F32))
    o = jnp.concatenate(heads, axis=-1).astype(BF16)
    cross = jnp.dot(o, wo_ref[...], preferred_element_type=F32)
    o_ref[...] = _layer_norm(ALPHA * h + cross, g3_ref[...], b3_ref[...])


def _mix_cross(h, ylru, ysb, g_sb, w_out, g2, b2, kt, v, wq, wo, g3, b3, seq):
    n = h.shape[0]
    tm = 2 * TOKEN_TILE
    per_b = seq // tm
    mlen = v.shape[1]
    vec = _const_spec((1, D_MODEL))
    mat = _const_spec((D_MODEL, D_MODEL))
    return pl.pallas_call(
        _mix_cross_kernel,
        grid=(n // tm,),
        in_specs=[
            pl.BlockSpec((tm, D_MODEL), lambda i: (i, 0)),
            pl.BlockSpec((tm, LRU_WIDTH), lambda i: (i, 0)),
            pl.BlockSpec((tm, SB_WIDTH), lambda i: (i, 0)),
            _const_spec((1, SB_WIDTH)),
            mat, vec, vec,
            pl.BlockSpec((1, D_MODEL, mlen), lambda i: (i // per_b, 0, 0)),
            pl.BlockSpec((1, mlen, D_MODEL), lambda i: (i // per_b, 0, 0)),
            mat, mat, vec, vec,
        ],
        out_specs=pl.BlockSpec((tm, D_MODEL), lambda i: (i, 0)),
        out_shape=jax.ShapeDtypeStruct((n, D_MODEL), F32),
        compiler_params=_cparams("parallel"),
        name="mix_cross",
    )(h, ylru, ysb, g_sb, w_out, g2, b2, kt, v, wq, wo, g3, b3)


def _block_diag(w):
    nh, d, _ = w.shape
    eye = jnp.eye(nh, dtype=w.dtype)
    return (eye[:, None, :, None] * w[:, :, None, :]).reshape(nh * d, nh * d)


def kernel(x, mem, ffn1_w13, ffn1_w2, ln1_g, ln1_b, w_in, conv_w, conv_b, w_rgate, b_rgate, w_igate, b_igate, lru_lambda, g_lru, g_sb, w_out, ln2_g, ln2_b, mem_wq, mem_wkv, mem_wo, ln3_g, ln3_b, ffn2_w13, ffn2_w2, ln4_g, ln4_b):
    bsz, seq, _ = x.shape
    assert seq % TOKEN_TILE == 0 and seq % LRU_CHUNK == 0 and seq % SB_QTILE == 0
    assert TOKEN_TILE % SB_BLOCK == 0 and SB_QTILE % SB_BLOCK == 0
    h = x.reshape(bsz * seq, D_MODEL)
    vec = lambda v: v.reshape(1, -1)
    for l in range(DEPTH):
        h = _ffn_ln(h, _ffn_w13(ffn1_w13[l]), ffn1_w2[l].astype(BF16), vec(ln1_g[l]), vec(ln1_b[l]))
        wxgq = w_in[l][:, :2 * LRU_WIDTH + SB_WIDTH].astype(BF16)
        wkv_t = w_in[l][:, 2 * LRU_WIDTH + SB_WIDTH:].T.astype(BF16)
        xg, q, kv = _mixer_in(h, wxgq, wkv_t, bsz, seq)
        ylru = _rglru(xg, conv_w[l], vec(conv_b[l]),
                      _block_diag(w_rgate[l]).astype(BF16), vec(b_rgate[l]),
                      _block_diag(w_igate[l]).astype(BF16), vec(b_igate[l]),
                      vec(lru_lambda[l]), vec(g_lru[l]), bsz, seq)
        ysb = _sb_attn(q, kv, bsz, seq).reshape(bsz * seq, SB_WIDTH)
        kt, v = _mem_kv(mem, mem_wkv[l][:, :D_MODEL].T.astype(BF16), mem_wkv[l][:, D_MODEL:].astype(BF16))
        h = _mix_cross(h, ylru, ysb, vec(g_sb[l]), w_out[l].astype(BF16), vec(ln2_g[l]), vec(ln2_b[l]),
                       kt, v, mem_wq[l].astype(BF16), mem_wo[l].astype(BF16), vec(ln3_g[l]), vec(ln3_b[l]), seq)
        h = _ffn_ln(h, _ffn_w13(ffn2_w13[l]), ffn2_w2[l].astype(BF16), vec(ln4_g[l]), vec(ln4_b[l]))
    return h.reshape(bsz, seq, D_MODEL)
```

```python
import math

import jax
import jax.numpy as jnp
from jax import lax
from jax.experimental import pallas as pl
from jax.experimental.pallas import tpu as pltpu

F32 = jnp.float32
BF16 = jnp.bfloat16

D_MODEL = 1024
DEPTH = 1
LRU_WIDTH = D_MODEL // 2
LRU_HEADS = 8
LRU_HEAD_DIM = LRU_WIDTH // LRU_HEADS
CONV_WIDTH = 4
LRU_C = 8.0
SB_WIDTH = D_MODEL - LRU_WIDTH
SB_HEADS = 8
SB_HEAD_DIM = SB_WIDTH // SB_HEADS
D_FF = 2688
MEM_HEADS = 4
MEM_HEAD_DIM = D_MODEL // MEM_HEADS
ALPHA = (2 * DEPTH) ** 0.25
LN_EPS = 1e-5
RMS_EPS = 1e-6

LOG2E = 1.4426950408889634
LANES = 128

VMEM_LIMIT_BYTES = 56 * 1024 * 1024

TOKEN_TILE = 512
FF_CHUNKS = (1280, 1408)
SB_BLOCK = 256
SB_QTILE = 512
SB_DEAD_LOG2 = 160.0
LRU_CHUNK = 512


def _cparams(*sem):
    return pltpu.CompilerParams(dimension_semantics=sem, vmem_limit_bytes=VMEM_LIMIT_BYTES)


def _const_spec(shape):
    nd = len(shape)
    return pl.BlockSpec(shape, lambda *_: (0,) * nd, pipeline_mode=pl.Buffered(1))


def _layer_norm(r, g, b):
    mu = jnp.mean(r, axis=-1, keepdims=True)
    d = r - mu
    var = jnp.mean(d * d, axis=-1, keepdims=True)
    return d * lax.rsqrt(var + LN_EPS) * g + b


def _sigmoid(x):
    return 0.5 * jnp.tanh(0.5 * x) + 0.5


def _ffn_ln_kernel(x_ref, w13_ref, w2_ref, g_ref, b_ref, o_ref):
    x = x_ref[...]
    xb = x.astype(BF16)
    y = None
    off = 0
    for width in FF_CHUNKS:
        gu = jnp.dot(xb, w13_ref[:, 2 * off:2 * (off + width)], preferred_element_type=F32)
        gate, up = gu[:, :width], gu[:, width:]
        act = (gate * _sigmoid(gate) * up).astype(BF16)
        part = jnp.dot(act, w2_ref[off:off + width, :], preferred_element_type=F32)
        y = part if y is None else y + part
        off += width
    o_ref[...] = _layer_norm(ALPHA * x + 0.5 * y, g_ref[...], b_ref[...])


def _ffn_w13(w13):
    parts, off = [], 0
    for width in FF_CHUNKS:
        parts += [w13[:, off:off + width], w13[:, D_FF + off:D_FF + off + width]]
        off += width
    return jnp.concatenate(parts, axis=1).astype(BF16)


def _ffn_ln(x, w13, w2, g, b):
    n = x.shape[0]
    tm = 2 * TOKEN_TILE
    return pl.pallas_call(
        _ffn_ln_kernel,
        grid=(n // tm,),
        in_specs=[
            pl.BlockSpec((tm, D_MODEL), lambda i: (i, 0)),
            _const_spec((D_MODEL, 2 * D_FF)),
            _const_spec((D_FF, D_MODEL)),
            _const_spec((1, D_MODEL)),
            _const_spec((1, D_MODEL)),
        ],
        out_specs=pl.BlockSpec((tm, D_MODEL), lambda i: (i, 0)),
        out_shape=jax.ShapeDtypeStruct((n, D_MODEL), F32),
        compiler_params=_cparams("parallel"),
        name="ffn_ln",
    )(x, w13, w2, g, b)


def _mixer_in_kernel(h_ref, wxgq_ref, wkv_t_ref, xg_ref, q_ref, kv_ref):
    hb = h_ref[...].astype(BF16)
    xgq = jnp.dot(hb, wxgq_ref[...], preferred_element_type=F32)
    xg_ref[...] = xgq[:, :2 * LRU_WIDTH]
    q_ref[...] = (xgq[:, 2 * LRU_WIDTH:] * (LOG2E / math.sqrt(SB_HEAD_DIM))).astype(BF16)
    t = lax.dot_general(wkv_t_ref[...], hb, (((1,), (1,)), ((), ())), preferred_element_type=F32)
    tm = hb.shape[0]
    for c in range(tm // SB_BLOCK):
        blk = t[:, c * SB_BLOCK:(c + 1) * SB_BLOCK]
        for j in range(2 * SB_HEADS):
            kv_ref[0, j, c] = blk[j * SB_HEAD_DIM:(j + 1) * SB_HEAD_DIM, :].astype(BF16)


def _mixer_in(h, wxgq, wkv_t, bsz, seq):
    n = h.shape[0]
    tm = TOKEN_TILE
    per_b = seq // tm
    nbt = tm // SB_BLOCK
    nh = 2 * SB_HEADS
    return pl.pallas_call(
        _mixer_in_kernel,
        grid=(n // tm,),
        in_specs=[
            pl.BlockSpec((tm, D_MODEL), lambda i: (i, 0)),
            _const_spec((D_MODEL, 2 * LRU_WIDTH + SB_WIDTH)),
            _const_spec((2 * SB_WIDTH, D_MODEL)),
        ],
        out_specs=[
            pl.BlockSpec((tm, 2 * LRU_WIDTH), lambda i: (i, 0)),
            pl.BlockSpec((tm, SB_WIDTH), lambda i: (i, 0)),
            pl.BlockSpec((1, nh, nbt, SB_HEAD_DIM, SB_BLOCK),
                         lambda i: (i // per_b, 0, i % per_b, 0, 0)),
        ],
        out_shape=[
            jax.ShapeDtypeStruct((n, 2 * LRU_WIDTH), F32),
            jax.ShapeDtypeStruct((n, SB_WIDTH), BF16),
            jax.ShapeDtypeStruct((bsz, nh, seq // SB_BLOCK, SB_HEAD_DIM, SB_BLOCK), BF16),
        ],
        compiler_params=_cparams("parallel"),
        name="mixer_in",
    )(h, wxgq, wkv_t)


def _rglru_kernel(xg_ref, cw_ref, cb_ref, wr_ref, br_ref, wi_ref, bi_ref, lam_ref, g_ref,
                  o_ref, xpad_ref, a_ref, u_ref, hs_ref, hcar_ref):
    ts = LRU_CHUNK
    c = pl.program_id(1)

    @pl.when(c == 0)
    def _():
        xpad_ref[0:8, :] = jnp.zeros((8, LRU_WIDTH), F32)
        hcar_ref[...] = jnp.zeros_like(hcar_ref)

    x = xg_ref[:, 0:LRU_WIDTH]
    gate = xg_ref[:, LRU_WIDTH:2 * LRU_WIDTH]
    xpad_ref[8:8 + ts, :] = x
    xc = cb_ref[...] + x * cw_ref[CONV_WIDTH - 1:CONV_WIDTH, :]
    for k in range(CONV_WIDTH - 1):
        back = CONV_WIDTH - 1 - k
        xc = xc + xpad_ref[8 - back:8 - back + ts, :] * cw_ref[k:k + 1, :]
    xpad_ref[0:8, :] = x[ts - 8:ts, :]

    xcb = xc.astype(BF16)
    r = _sigmoid(jnp.dot(xcb, wr_ref[...], preferred_element_type=F32) + br_ref[...])
    i = _sigmoid(jnp.dot(xcb, wi_ref[...], preferred_element_type=F32) + bi_ref[...])
    nlam = -lam_ref[...]
    softplus = jnp.maximum(nlam, 0.0) + jnp.log1p(jnp.exp(-jnp.abs(nlam)))
    log_a = (-LRU_C) * r * softplus
    a_ref[...] = jnp.exp(log_a)
    th = jnp.tanh(log_a)
    n = -2.0 * th
    root = jnp.where(n > 0.0, n * lax.rsqrt(n * (1.0 - th)), 0.0)
    u_ref[...] = root * (i * xc)

    def step(t, h):
        h = a_ref[pl.ds(t, 1), :] * h + u_ref[pl.ds(t, 1), :]
        hs_ref[pl.ds(t, 1), :] = h
        return h

    hcar_ref[...] = lax.fori_loop(0, ts, step, hcar_ref[...], unroll=8)

    gelu = 0.5 * gate * (1.0 + jnp.tanh(math.sqrt(2.0 / math.pi) * (gate + 0.044715 * (gate * gate * gate))))
    y = hs_ref[...] * gelu
    ms = jnp.mean(y * y, axis=-1, keepdims=True)
    o_ref[...] = (y * lax.rsqrt(ms + RMS_EPS) * g_ref[...]).astype(BF16)


def _rglru(xg, conv_w, conv_b, wr, br, wi, bi, lam, g_lru, bsz, seq):
    n = xg.shape[0]
    ts = LRU_CHUNK
    per_b = seq // ts
    row = lambda b, c: (b * per_b + c, 0)
    vec = _const_spec((1, LRU_WIDTH))
    return pl.pallas_call(
        _rglru_kernel,
        grid=(bsz, per_b),
        in_specs=[
            pl.BlockSpec((ts, 2 * LRU_WIDTH), row),
            _const_spec((CONV_WIDTH, LRU_WIDTH)), vec,
            _const_spec((LRU_WIDTH, LRU_WIDTH)), vec,
            _const_spec((LRU_WIDTH, LRU_WIDTH)), vec,
            vec, vec,
        ],
        out_specs=pl.BlockSpec((ts, LRU_WIDTH), row),
        out_shape=jax.ShapeDtypeStruct((n, LRU_WIDTH), BF16),
        scratch_shapes=[
            pltpu.VMEM((ts + 8, LRU_WIDTH), F32),
            pltpu.VMEM((ts, LRU_WIDTH), F32),
            pltpu.VMEM((ts, LRU_WIDTH), F32),
            pltpu.VMEM((ts, LRU_WIDTH), F32),
            pltpu.VMEM((1, LRU_WIDTH), F32),
        ],
        compiler_params=_cparams("parallel", "arbitrary"),
        name="rglru",
    )(xg, conv_w, conv_b, wr, br, wi, bi, lam, g_lru)


def _sb_head_block(q, kt, vt_pad, tri, carry, causal):
    z = jnp.dot(q, kt, preferred_element_type=F32)
    neg_abs = lax.bitcast_convert_type(lax.bitcast_convert_type(z, jnp.int32) | jnp.int32(-2 ** 31), F32)
    p = jnp.maximum(z, 0.0) + jnp.log2(1.0 + jnp.exp2(neg_abs))
    if causal is not None:
        p = jnp.where(causal, p, 0.0)
    c = jnp.dot(p.astype(BF16), tri, preferred_element_type=F32)
    d = jnp.minimum(z - c, 0.0)
    w = jnp.exp2(d - jnp.concatenate([carry] * (z.shape[1] // LANES), axis=1))
    if causal is not None:
        w = jnp.where(causal, w, 0.0)
    pv = lax.dot_general(w.astype(BF16), vt_pad, (((1,), (1,)), ((), ())), preferred_element_type=F32)
    return carry + jnp.broadcast_to(c[:, 0:1], carry.shape), pv


def _sb_attn_kernel(q_ref, k_ref, v_ref, o_ref, acc_ref):
    tb, tq = SB_BLOCK, SB_QTILE
    r = tq // tb
    qi = pl.program_id(2)
    tri_row = lax.broadcasted_iota(jnp.int32, (tb, tb), 0)
    tri_col = lax.broadcasted_iota(jnp.int32, (tb, tb), 1)
    tri = (tri_row >= tri_col).astype(BF16)
    zpad = jnp.zeros((SB_HEAD_DIM, tb), BF16)
    q = q_ref[...]

    def step(kb, carries, lo, causal, keep=None):
        new, total = [], None
        for hh in range(2):
            pad = (lambda x: jnp.concatenate([x, zpad], axis=0)) if hh == 0 else (
                lambda x: jnp.concatenate([zpad, x], axis=0))
            cnew, pv = _sb_head_block(q[lo:], pad(k_ref[0, hh, kb]), pad(v_ref[0, hh, kb]), tri,
                                      carries[hh][lo:], causal)
            new.append(cnew if lo == 0 else jnp.concatenate([carries[hh][:lo], cnew], axis=0))
            total = pv if total is None else total + pv
        if keep is not None:
            total = jnp.where(keep, total, 0.0)
        acc_ref[lo:, :] += total
        return tuple(new)

    acc_ref[...] = jnp.zeros_like(acc_ref)
    carries = (jnp.zeros((tq, LANES), F32), jnp.zeros((tq, LANES), F32))
    for m in reversed(range(r)):
        lo = m * tb
        row = lax.broadcasted_iota(jnp.int32, (tq - lo, tb), 0)
        col = lax.broadcasted_iota(jnp.int32, (tq - lo, tb), 1)
        carries = step(r * qi + m, carries, lo, col < row)

    n_left = r * qi
    carries = step(jnp.maximum(n_left - 1, 0), carries, 0, None, keep=n_left > 0)

    def body(state):
        j, carries = state[0], state[1:]
        return (j + 1,) + step(n_left - 1 - j, carries, 0, None)

    def more(state):
        j, c0, c1 = state
        return jnp.logical_and(j < n_left, jnp.min(jnp.minimum(c0, c1)) < SB_DEAD_LOG2)

    lax.while_loop(more, body, (jnp.int32(1),) + carries)
    o_ref[0] = acc_ref[...]


def _sb_attn(q, kv, bsz, seq):
    tb, tq = SB_BLOCK, SB_QTILE
    nb = seq // tb
    hp = SB_HEADS // 2
    kv_block = (1, 2, nb, SB_HEAD_DIM, tb)
    return pl.pallas_call(
        _sb_attn_kernel,
        grid=(bsz, hp, seq // tq),
        in_specs=[
            pl.BlockSpec((tq, 2 * SB_HEAD_DIM), lambda b, h, i: (b * (seq // tq) + i, h)),
            pl.BlockSpec(kv_block, lambda b, h, i: (b, h, 0, 0, 0)),
            pl.BlockSpec(kv_block, lambda b, h, i: (b, hp + h, 0, 0, 0)),
        ],
        out_specs=pl.BlockSpec((1, tq, 2 * SB_HEAD_DIM), lambda b, h, i: (b, i, h)),
        out_shape=jax.ShapeDtypeStruct((bsz, seq, SB_WIDTH), F32),
        scratch_shapes=[pltpu.VMEM((tq, 2 * SB_HEAD_DIM), F32)],
        compiler_params=_cparams("parallel", "parallel", "arbitrary"),
        name="sb_attn",
    )(q, kv, kv)


def _mem_kv_kernel(mem_ref, wk_t_ref, wv_ref, kt_ref, v_ref):
    mb = mem_ref[0].astype(BF16)
    kt = lax.dot_general(wk_t_ref[...], mb, (((1,), (1,)), ((), ())), preferred_element_type=F32)
    kt_ref[0] = kt.astype(BF16)
    v_ref[0] = jnp.dot(mb, wv_ref[...], preferred_element_type=F32).astype(BF16)


def _mem_kv(mem, wk_t, wv):
    bsz, mlen, _ = mem.shape
    return pl.pallas_call(
        _mem_kv_kernel,
        grid=(bsz,),
        in_specs=[
            pl.BlockSpec((1, mlen, D_MODEL), lambda b: (b, 0, 0)),
            _const_spec((D_MODEL, D_MODEL)),
            _const_spec((D_MODEL, D_MODEL)),
        ],
        out_specs=[
            pl.BlockSpec((1, D_MODEL, mlen), lambda b: (b, 0, 0)),
            pl.BlockSpec((1, mlen, D_MODEL), lambda b: (b, 0, 0)),
        ],
        out_shape=[
            jax.ShapeDtypeStruct((bsz, D_MODEL, mlen), BF16),
            jax.ShapeDtypeStruct((bsz, mlen, D_MODEL), BF16),
        ],
        compiler_params=_cparams("parallel"),
        name="mem_kv",
    )(mem, wk_t, wv)


def _mix_cross_kernel(h_ref, ylru_ref, ysb_ref, gsb_ref, wout_ref, g2_ref, b2_ref,
                      kt_ref, v_ref, wq_ref, wo_ref, g3_ref, b3_ref, o_ref):
    ysb = ysb_ref[...]
    ms = jnp.mean(ysb * ysb, axis=-1, keepdims=True)
    ysb_n = (ysb * lax.rsqrt(ms + RMS_EPS) * gsb_ref[...]).astype(BF16)
    mix = (jnp.dot(ylru_ref[...], wout_ref[0:LRU_WIDTH, :], preferred_element_type=F32)
           + jnp.dot(ysb_n, wout_ref[LRU_WIDTH:, :], preferred_element_type=F32))
    h = _layer_norm(ALPHA * h_ref[...] + mix, g2_ref[...], b2_ref[...])
    q = jnp.dot(h.astype(BF16), wq_ref[...], preferred_element_type=F32)
    q = (q * (1.0 / math.sqrt(MEM_HEAD_DIM))).astype(BF16)
    heads = []
    for hd in range(MEM_HEADS):
        sl = slice(hd * MEM_HEAD_DIM, (hd + 1) * MEM_HEAD_DIM)
        s = jnp.dot(q[:, sl], kt_ref[0, sl, :], preferred_element_type=F32)
        e = jnp.exp(s - jnp.max(s, axis=-1, keepdims=True))
        p = e * (1.0 / jnp.sum(e, axis=-1, keepdims=True))
        heads.append(jnp.dot(p.astype(BF16), v_ref[0, :, sl], preferred_element_type=F32))
    o = jnp.concatenate(heads, axis=-1).astype(BF16)
    cross = jnp.dot(o, wo_ref[...], preferred_element_type=F32)
    o_ref[...] = _layer_norm(ALPHA * h + cross, g3_ref[...], b3_ref[...])


def _mix_cross(h, ylru, ysb, g_sb, w_out, g2, b2, kt, v, wq, wo, g3, b3, seq):
    n = h.shape[0]
    tm = 2 * TOKEN_TILE
    per_b = seq // tm
    mlen = v.shape[1]
    vec = _const_spec((1, D_MODEL))
    mat = _const_spec((D_MODEL, D_MODEL))
    return pl.pallas_call(
        _mix_cross_kernel,
        grid=(n // tm,),
        in_specs=[
            pl.BlockSpec((tm, D_MODEL), lambda i: (i, 0)),
            pl.BlockSpec((tm, LRU_WIDTH), lambda i: (i, 0)),
            pl.BlockSpec((tm, SB_WIDTH), lambda i: (i, 0)),
            _const_spec((1, SB_WIDTH)),
            mat, vec, vec,
            pl.BlockSpec((1, D_MODEL, mlen), lambda i: (i // per_b, 0, 0)),
            pl.BlockSpec((1, mlen, D_MODEL), lambda i: (i // per_b, 0, 0)),
            mat, mat, vec, vec,
        ],
        out_specs=pl.BlockSpec((tm, D_MODEL), lambda i: (i, 0)),
        out_shape=jax.ShapeDtypeStruct((n, D_MODEL), F32),
        compiler_params=_cparams("parallel"),
        name="mix_cross",
    )(h, ylru, ysb, g_sb, w_out, g2, b2, kt, v, wq, wo, g3, b3)


def _block_diag(w):
    nh, d, _ = w.shape
    eye = jnp.eye(nh, dtype=w.dtype)
    return (eye[:, None, :, None] * w[:, :, None, :]).reshape(nh * d, nh * d)


def kernel(x, mem, ffn1_w13, ffn1_w2, ln1_g, ln1_b, w_in, conv_w, conv_b, w_rgate, b_rgate, w_igate, b_igate, lru_lambda, g_lru, g_sb, w_out, ln2_g, ln2_b, mem_wq, mem_wkv, mem_wo, ln3_g, ln3_b, ffn2_w13, ffn2_w2, ln4_g, ln4_b):
    bsz, seq, _ = x.shape
    assert seq % TOKEN_TILE == 0 and seq % LRU_CHUNK == 0 and seq % SB_QTILE == 0
    assert TOKEN_TILE % SB_BLOCK == 0 and SB_QTILE % SB_BLOCK == 0
    h = x.reshape(bsz * seq, D_MODEL)
    vec = lambda v: v.reshape(1, -1)
    for l in range(DEPTH):
        h = _ffn_ln(h, _ffn_w13(ffn1_w13[l]), ffn1_w2[l].astype(BF16), vec(ln1_g[l]), vec(ln1_b[l]))
        wxgq = w_in[l][:, :2 * LRU_WIDTH + SB_WIDTH].astype(BF16)
        wkv_t = w_in[l][:, 2 * LRU_WIDTH + SB_WIDTH:].T.astype(BF16)
        xg, q, kv = _mixer_in(h, wxgq, wkv_t, bsz, seq)
        ylru = _rglru(xg, conv_w[l], vec(conv_b[l]),
                      _block_diag(w_rgate[l]).astype(BF16), vec(b_rgate[l]),
                      _block_diag(w_igate[l]).astype(BF16), vec(b_igate[l]),
                      vec(lru_lambda[l]), vec(g_lru[l]), bsz, seq)
        ysb = _sb_attn(q, kv, bsz, seq).reshape(bsz * seq, SB_WIDTH)
        kt, v = _mem_kv(mem, mem_wkv[l][:, :D_MODEL].T.astype(BF16), mem_wkv[l][:, D_MODEL:].astype(BF16))
        h = _mix_cross(h, ylru, ysb, vec(g_sb[l]), w_out[l].astype(BF16), vec(ln2_g[l]), vec(ln2_b[l]),
                       kt, v, mem_wq[l].astype(BF16), mem_wo[l].astype(BF16), vec(ln3_g[l]), vec(ln3_b[l]), seq)
        h = _ffn_ln(h, _ffn_w13(ffn2_w13[l]), ffn2_w2[l].astype(BF16), vec(ln4_g[l]), vec(ln4_b[l]))
    return h.reshape(bsz, seq, D_MODEL)
```

```python
import math

import jax
import jax.numpy as jnp
from jax import lax
from jax.experimental import pallas as pl
from jax.experimental.pallas import tpu as pltpu

F32 = jnp.float32
BF16 = jnp.bfloat16

D_MODEL = 1024
DEPTH = 1
LRU_WIDTH = D_MODEL // 2
LRU_HEADS = 8
LRU_HEAD_DIM = LRU_WIDTH // LRU_HEADS
CONV_WIDTH = 4
LRU_C = 8.0
SB_WIDTH = D_MODEL - LRU_WIDTH
SB_HEADS = 8
SB_HEAD_DIM = SB_WIDTH // SB_HEADS
D_FF = 2688
MEM_HEADS = 4
MEM_HEAD_DIM = D_MODEL // MEM_HEADS
ALPHA = (2 * DEPTH) ** 0.25
LN_EPS = 1e-5
RMS_EPS = 1e-6

LOG2E = 1.4426950408889634
LANES = 128

VMEM_LIMIT_BYTES = 56 * 1024 * 1024

TOKEN_TILE = 512
FF_CHUNKS = (1280, 1408)
SB_BLOCK = 256
SB_QTILE = 512
SB_DEAD_LOG2 = 160.0
LRU_CHUNK = 512


def _cparams(*sem):
    return pltpu.CompilerParams(dimension_semantics=sem, vmem_limit_bytes=VMEM_LIMIT_BYTES)


def _const_spec(shape):
    nd = len(shape)
    return pl.BlockSpec(shape, lambda *_: (0,) * nd, pipeline_mode=pl.Buffered(1))


def _layer_norm(r, g, b):
    mu = jnp.mean(r, axis=-1, keepdims=True)
    d = r - mu
    var = jnp.mean(d * d, axis=-1, keepdims=True)
    return d * lax.rsqrt(var + LN_EPS) * g + b


def _sigmoid(x):
    return 0.5 * jnp.tanh(0.5 * x) + 0.5


def _ffn_ln_kernel(x_ref, w13_ref, w2_ref, g_ref, b_ref, o_ref):
    x = x_ref[...]
    xb = x.astype(BF16)
    y = None
    off = 0
    for width in FF_CHUNKS:
        gu = jnp.dot(xb, w13_ref[:, 2 * off:2 * (off + width)], preferred_element_type=F32)
        gate, up = gu[:, :width], gu[:, width:]
        act = (gate * _sigmoid(gate) * up).astype(BF16)
        part = jnp.dot(act, w2_ref[off:off + width, :], preferred_element_type=F32)
        y = part if y is None else y + part
        off += width
    o_ref[...] = _layer_norm(ALPHA * x + 0.5 * y, g_ref[...], b_ref[...])


def _ffn_w13(w13):
    parts, off = [], 0
    for width in FF_CHUNKS:
        parts += [w13[:, off:off + width], w13[:, D_FF + off:D_FF + off + width]]
        off += width
    return jnp.concatenate(parts, axis=1).astype(BF16)


def _ffn_ln(x, w13, w2, g, b):
    n = x.shape[0]
    tm = 2 * TOKEN_TILE
    return pl.pallas_call(
        _ffn_ln_kernel,
        grid=(n // tm,),
        in_specs=[
            pl.BlockSpec((tm, D_MODEL), lambda i: (i, 0)),
            _const_spec((D_MODEL, 2 * D_FF)),
            _const_spec((D_FF, D_MODEL)),
            _const_spec((1, D_MODEL)),
            _const_spec((1, D_MODEL)),
        ],
        out_specs=pl.BlockSpec((tm, D_MODEL), lambda i: (i, 0)),
        out_shape=jax.ShapeDtypeStruct((n, D_MODEL), F32),
        compiler_params=_cparams("parallel"),
        name="ffn_ln",
    )(x, w13, w2, g, b)


def _mixer_in_kernel(h_ref, wxgq_ref, wkv_t_ref, xg_ref, q_ref, kv_ref):
    hb = h_ref[...].astype(BF16)
    xgq = jnp.dot(hb, wxgq_ref[...], preferred_element_type=F32)
    xg_ref[...] = xgq[:, :2 * LRU_WIDTH]
    q_ref[...] = (xgq[:, 2 * LRU_WIDTH:] * (LOG2E / math.sqrt(SB_HEAD_DIM))).astype(BF16)
    t = lax.dot_general(wkv_t_ref[...], hb, (((1,), (1,)), ((), ())), preferred_element_type=F32)
    tm = hb.shape[0]
    for c in range(tm // SB_BLOCK):
        blk = t[:, c * SB_BLOCK:(c + 1) * SB_BLOCK]
        for j in range(2 * SB_HEADS):
            kv_ref[0, j, c] = blk[j * SB_HEAD_DIM:(j + 1) * SB_HEAD_DIM, :].astype(BF16)


def _mixer_in(h, wxgq, wkv_t, bsz, seq):
    n = h.shape[0]
    tm = TOKEN_TILE
    per_b = seq // tm
    nbt = tm // SB_BLOCK
    nh = 2 * SB_HEADS
    return pl.pallas_call(
        _mixer_in_kernel,
        grid=(n // tm,),
        in_specs=[
            pl.BlockSpec((tm, D_MODEL), lambda i: (i, 0)),
            _const_spec((D_MODEL, 2 * LRU_WIDTH + SB_WIDTH)),
            _const_spec((2 * SB_WIDTH, D_MODEL)),
        ],
        out_specs=[
            pl.BlockSpec((tm, 2 * LRU_WIDTH), lambda i: (i, 0)),
            pl.BlockSpec((tm, SB_WIDTH), lambda i: (i, 0)),
            pl.BlockSpec((1, nh, nbt, SB_HEAD_DIM, SB_BLOCK),
                         lambda i: (i // per_b, 0, i % per_b, 0, 0)),
        ],
        out_shape=[
            jax.ShapeDtypeStruct((n, 2 * LRU_WIDTH), F32),
            jax.ShapeDtypeStruct((n, SB_WIDTH), BF16),
            jax.ShapeDtypeStruct((bsz, nh, seq // SB_BLOCK, SB_HEAD_DIM, SB_BLOCK), BF16),
        ],
        compiler_params=_cparams("parallel"),
        name="mixer_in",
    )(h, wxgq, wkv_t)


def _rglru_kernel(xg_ref, cw_ref, cb_ref, wr_ref, br_ref, wi_ref, bi_ref, lam_ref, g_ref,
                  o_ref, xpad_ref, a_ref, u_ref, hs_ref, hcar_ref):
    ts = LRU_CHUNK
    c = pl.program_id(1)

    @pl.when(c == 0)
    def _():
        xpad_ref[0:8, :] = jnp.zeros((8, LRU_WIDTH), F32)
        hcar_ref[...] = jnp.zeros_like(hcar_ref)

    x = xg_ref[:, 0:LRU_WIDTH]
    gate = xg_ref[:, LRU_WIDTH:2 * LRU_WIDTH]
    xpad_ref[8:8 + ts, :] = x
    xc = cb_ref[...] + x * cw_ref[CONV_WIDTH - 1:CONV_WIDTH, :]
    for k in range(CONV_WIDTH - 1):
        back = CONV_WIDTH - 1 - k
        xc = xc + xpad_ref[8 - back:8 - back + ts, :] * cw_ref[k:k + 1, :]
    xpad_ref[0:8, :] = x[ts - 8:ts, :]

    xcb = xc.astype(BF16)
    r = _sigmoid(jnp.dot(xcb, wr_ref[...], preferred_element_type=F32) + br_ref[...])
    i = _sigmoid(jnp.dot(xcb, wi_ref[...], preferred_element_type=F32) + bi_ref[...])
    nlam = -lam_ref[...]
    softplus = jnp.maximum(nlam, 0.0) + jnp.log1p(jnp.exp(-jnp.abs(nlam)))
    log_a = (-LRU_C) * r * softplus
    a_ref[...] = jnp.exp(log_a)
    th = jnp.tanh(log_a)
    n = -2.0 * th
    root = jnp.where(n > 0.0, n * lax.rsqrt(n * (1.0 - th)), 0.0)
    u_ref[...] = root * (i * xc)

    def step(t, h):
        h = a_ref[pl.ds(t, 1), :] * h + u_ref[pl.ds(t, 1), :]
        hs_ref[pl.ds(t, 1), :] = h
        return h

    hcar_ref[...] = lax.fori_loop(0, ts, step, hcar_ref[...], unroll=8)

    gelu = 0.5 * gate * (1.0 + jnp.tanh(math.sqrt(2.0 / math.pi) * (gate + 0.044715 * (gate * gate * gate))))
    y = hs_ref[...] * gelu
    ms = jnp.mean(y * y, axis=-1, keepdims=True)
    o_ref[...] = (y * lax.rsqrt(ms + RMS_EPS) * g_ref[...]).astype(BF16)


def _rglru(xg, conv_w, conv_b, wr, br, wi, bi, lam, g_lru, bsz, seq):
    n = xg.shape[0]
    ts = LRU_CHUNK
    per_b = seq // ts
    row = lambda b, c: (b * per_b + c, 0)
    vec = _const_spec((1, LRU_WIDTH))
    return pl.pallas_call(
        _rglru_kernel,
        grid=(bsz, per_b),
        in_specs=[
            pl.BlockSpec((ts, 2 * LRU_WIDTH), row),
            _const_spec((CONV_WIDTH, LRU_WIDTH)), vec,
            _const_spec((LRU_WIDTH, LRU_WIDTH)), vec,
            _const_spec((LRU_WIDTH, LRU_WIDTH)), vec,
            vec, vec,
        ],
        out_specs=pl.BlockSpec((ts, LRU_WIDTH), row),
        out_shape=jax.ShapeDtypeStruct((n, LRU_WIDTH), BF16),
        scratch_shapes=[
            pltpu.VMEM((ts + 8, LRU_WIDTH), F32),
            pltpu.VMEM((ts, LRU_WIDTH), F32),
            pltpu.VMEM((ts, LRU_WIDTH), F32),
            pltpu.VMEM((ts, LRU_WIDTH), F32),
            pltpu.VMEM((1, LRU_WIDTH), F32),
        ],
        compiler_params=_cparams("parallel", "arbitrary"),
        name="rglru",
    )(xg, conv_w, conv_b, wr, br, wi, bi, lam, g_lru)


def _sb_head_block(q, kt, vt_pad, tri, carry, causal):
    z = jnp.dot(q, kt, preferred_element_type=F32)
    neg_abs = lax.bitcast_convert_type(lax.bitcast_convert_type(z, jnp.int32) | jnp.int32(-2 ** 31), F32)
    p = jnp.maximum(z, 0.0) + jnp.log2(1.0 + jnp.exp2(neg_abs))
    if causal is not None:
        p = jnp.where(causal, p, 0.0)
    c = jnp.dot(p.astype(BF16), tri, preferred_element_type=F32)
    d = jnp.minimum(z - c, 0.0)
    w = jnp.exp2(d - jnp.concatenate([carry] * (z.shape[1] // LANES), axis=1))
    if causal is not None:
        w = jnp.where(causal, w, 0.0)
    pv = lax.dot_general(w.astype(BF16), vt_pad, (((1,), (1,)), ((), ())), preferred_element_type=F32)
    return carry + jnp.broadcast_to(c[:, 0:1], carry.shape), pv


def _sb_attn_kernel(q_ref, k_ref, v_ref, o_ref, acc_ref):
    tb, tq = SB_BLOCK, SB_QTILE
    r = tq // tb
    qi = pl.program_id(2)
    tri_row = lax.broadcasted_iota(jnp.int32, (tb, tb), 0)
    tri_col = lax.broadcasted_iota(jnp.int32, (tb, tb), 1)
    tri = (tri_row >= tri_col).astype(BF16)
    zpad = jnp.zeros((SB_HEAD_DIM, tb), BF16)
    q = q_ref[...]

    def step(kb, carries, lo, causal, keep=None, hi=tq):
        new, total = [], None
        for hh in range(2):
            pad = (lambda x: jnp.concatenate([x, zpad], axis=0)) if hh == 0 else (
                lambda x: jnp.concatenate([zpad, x], axis=0))
            cold = carries[hh][lo:hi]
            cnew, pv = _sb_head_block(q[lo:hi], pad(k_ref[0, hh, kb]), pad(v_ref[0, hh, kb]), tri, cold, causal)
            if keep is not None:
                cnew = jnp.where(keep, cnew, cold)
            pieces = ([carries[hh][:lo]] if lo else []) + [cnew] + ([carries[hh][hi:]] if hi < tq else [])
            new.append(pieces[0] if len(pieces) == 1 else jnp.concatenate(pieces, axis=0))
            total = pv if total is None else total + pv
        if keep is not None:
            total = jnp.where(keep, total, 0.0)
        acc_ref[lo:hi, :] += total
        return tuple(new)

    acc_ref[...] = jnp.zeros_like(acc_ref)
    carries = (jnp.zeros((tq, LANES), F32), jnp.zeros((tq, LANES), F32))
    for m in reversed(range(r)):
        lo = m * tb
        row = lax.broadcasted_iota(jnp.int32, (tq - lo, tb), 0)
        col = lax.broadcasted_iota(jnp.int32, (tq - lo, tb), 1)
        carries = step(r * qi + m, carries, lo, col < row)

    n_left = r * qi
    carries = step(jnp.maximum(n_left - 1, 0), carries, 0, None, keep=n_left > 0, hi=tb)
    below_first = lax.broadcasted_iota(jnp.int32, (tq, LANES), 0) >= tb

    def body(state):
        j, carries = state[0], state[1:]
        return (j + 1,) + step(n_left - 1 - j, carries, 0, None, keep=jnp.logical_or(j > 0, below_first))

    def more(state):
        j, c0, c1 = state
        return jnp.logical_and(j < n_left, jnp.min(jnp.minimum(c0, c1)) < SB_DEAD_LOG2)

    lax.while_loop(more, body, (jnp.int32(0),) + carries)
    o_ref[0] = acc_ref[...]


def _sb_attn(q, kv, bsz, seq):
    tb, tq = SB_BLOCK, SB_QTILE
    nb = seq // tb
    hp = SB_HEADS // 2
    kv_block = (1, 2, nb, SB_HEAD_DIM, tb)
    return pl.pallas_call(
        _sb_attn_kernel,
        grid=(bsz, hp, seq // tq),
        in_specs=[
            pl.BlockSpec((tq, 2 * SB_HEAD_DIM), lambda b, h, i: (b * (seq // tq) + i, h)),
            pl.BlockSpec(kv_block, lambda b, h, i: (b, h, 0, 0, 0)),
            pl.BlockSpec(kv_block, lambda b, h, i: (b, hp + h, 0, 0, 0)),
        ],
        out_specs=pl.BlockSpec((1, tq, 2 * SB_HEAD_DIM), lambda b, h, i: (b, i, h)),
        out_shape=jax.ShapeDtypeStruct((bsz, seq, SB_WIDTH), F32),
        scratch_shapes=[pltpu.VMEM((tq, 2 * SB_HEAD_DIM), F32)],
        compiler_params=_cparams("parallel", "parallel", "arbitrary"),
        name="sb_attn",
    )(q, kv, kv)


def _mem_kv_kernel(mem_ref, wk_t_ref, wv_ref, kt_ref, v_ref):
    mb = mem_ref[0].astype(BF16)
    kt = lax.dot_general(wk_t_ref[...], mb, (((1,), (1,)), ((), ())), preferred_element_type=F32)
    kt_ref[0] = kt.astype(BF16)
    v_ref[0] = jnp.dot(mb, wv_ref[...], preferred_element_type=F32).astype(BF16)


def _mem_kv(mem, wk_t, wv):
    bsz, mlen, _ = mem.shape
    return pl.pallas_call(
        _mem_kv_kernel,
        grid=(bsz,),
        in_specs=[
            pl.BlockSpec((1, mlen, D_MODEL), lambda b: (b, 0, 0)),
            _const_spec((D_MODEL, D_MODEL)),
            _const_spec((D_MODEL, D_MODEL)),
        ],
        out_specs=[
            pl.BlockSpec((1, D_MODEL, mlen), lambda b: (b, 0, 0)),
            pl.BlockSpec((1, mlen, D_MODEL), lambda b: (b, 0, 0)),
        ],
        out_shape=[
            jax.ShapeDtypeStruct((bsz, D_MODEL, mlen), BF16),
            jax.ShapeDtypeStruct((bsz, mlen, D_MODEL), BF16),
        ],
        compiler_params=_cparams("parallel"),
        name="mem_kv",
    )(mem, wk_t, wv)


def _mix_cross_kernel(h_ref, ylru_ref, ysb_ref, gsb_ref, wout_ref, g2_ref, b2_ref,
                      kt_ref, v_ref, wq_ref, wo_ref, g3_ref, b3_ref, o_ref):
    ysb = ysb_ref[...]
    ms = jnp.mean(ysb * ysb, axis=-1, keepdims=True)
    ysb_n = (ysb * lax.rsqrt(ms + RMS_EPS) * gsb_ref[...]).astype(BF16)
    mix = (jnp.dot(ylru_ref[...], wout_ref[0:LRU_WIDTH, :], preferred_element_type=F32)
           + jnp.dot(ysb_n, wout_ref[LRU_WIDTH:, :], preferred_element_type=F32))
    h = _layer_norm(ALPHA * h_ref[...] + mix, g2_ref[...], b2_ref[...])
    q = jnp.dot(h.astype(BF16), wq_ref[...], preferred_element_type=F32)
    q = (q * (1.0 / math.sqrt(MEM_HEAD_DIM))).astype(BF16)
    heads = []
    for hd in range(MEM_HEADS):
        sl = slice(hd * MEM_HEAD_DIM, (hd + 1) * MEM_HEAD_DIM)
        s = jnp.dot(q[:, sl], kt_ref[0, sl, :], preferred_element_type=F32)
        e = jnp.exp(s - jnp.max(s, axis=-1, keepdims=True))
        p = e * (1.0 / jnp.sum(e, axis=-1, keepdims=True))
        heads.append(jnp.dot(p.astype(BF16), v_ref[0, :, sl], preferred_element_type=F32))
    o = jnp.concatenate(heads, axis=-1).astype(BF16)
    cross = jnp.dot(o, wo_ref[...], preferred_element_type=F32)
    o_ref[...] = _layer_norm(ALPHA * h + cross, g3_ref[...], b3_ref[...])


def _mix_cross(h, ylru, ysb, g_sb, w_out, g2, b2, kt, v, wq, wo, g3, b3, seq):
    n = h.shape[0]
    tm = 2 * TOKEN_TILE
    per_b = seq // tm
    mlen = v.shape[1]
    vec = _const_spec((1, D_MODEL))
    mat = _const_spec((D_MODEL, D_MODEL))
    return pl.pallas_call(
        _mix_cross_kernel,
        grid=(n // tm,),
        in_specs=[
            pl.BlockSpec((tm, D_MODEL), lambda i: (i, 0)),
            pl.BlockSpec((tm, LRU_WIDTH), lambda i: (i, 0)),
            pl.BlockSpec((tm, SB_WIDTH), lambda i: (i, 0)),
            _const_spec((1, SB_WIDTH)),
            mat, vec, vec,
            pl.BlockSpec((1, D_MODEL, mlen), lambda i: (i // per_b, 0, 0)),
            pl.BlockSpec((1, mlen, D_MODEL), lambda i: (i // per_b, 0, 0)),
            mat, mat, vec, vec,
        ],
        out_specs=pl.BlockSpec((tm, D_MODEL), lambda i: (i, 0)),
        out_shape=jax.ShapeDtypeStruct((n, D_MODEL), F32),
        compiler_params=_cparams("parallel"),
        name="mix_cross",
    )(h, ylru, ysb, g_sb, w_out, g2, b2, kt, v, wq, wo, g3, b3)


def _block_diag(w):
    nh, d, _ = w.shape
    eye = jnp.eye(nh, dtype=w.dtype)
    return (eye[:, None, :, None] * w[:, :, None, :]).reshape(nh * d, nh * d)


def kernel(x, mem, ffn1_w13, ffn1_w2, ln1_g, ln1_b, w_in, conv_w, conv_b, w_rgate, b_rgate, w_igate, b_igate, lru_lambda, g_lru, g_sb, w_out, ln2_g, ln2_b, mem_wq, mem_wkv, mem_wo, ln3_g, ln3_b, ffn2_w13, ffn2_w2, ln4_g, ln4_b):
    bsz, seq, _ = x.shape
    assert seq % TOKEN_TILE == 0 and seq % LRU_CHUNK == 0 and seq % SB_QTILE == 0
    assert TOKEN_TILE % SB_BLOCK == 0 and SB_QTILE % SB_BLOCK == 0
    h = x.reshape(bsz * seq, D_MODEL)
    vec = lambda v: v.reshape(1, -1)
    for l in range(DEPTH):
        h = _ffn_ln(h, _ffn_w13(ffn1_w13[l]), ffn1_w2[l].astype(BF16), vec(ln1_g[l]), vec(ln1_b[l]))
        wxgq = w_in[l][:, :2 * LRU_WIDTH + SB_WIDTH].astype(BF16)
        wkv_t = w_in[l][:, 2 * LRU_WIDTH + SB_WIDTH:].T.astype(BF16)
        xg, q, kv = _mixer_in(h, wxgq, wkv_t, bsz, seq)
        ylru = _rglru(xg, conv_w[l], vec(conv_b[l]),
                      _block_diag(w_rgate[l]).astype(BF16), vec(b_rgate[l]),
                      _block_diag(w_igate[l]).astype(BF16), vec(b_igate[l]),
                      vec(lru_lambda[l]), vec(g_lru[l]), bsz, seq)
        ysb = _sb_attn(q, kv, bsz, seq).reshape(bsz * seq, SB_WIDTH)
        kt, v = _mem_kv(mem, mem_wkv[l][:, :D_MODEL].T.astype(BF16), mem_wkv[l][:, D_MODEL:].astype(BF16))
        h = _mix_cross(h, ylru, ysb, vec(g_sb[l]), w_out[l].astype(BF16), vec(ln2_g[l]), vec(ln2_b[l]),
                       kt, v, mem_wq[l].astype(BF16), mem_wo[l].astype(BF16), vec(ln3_g[l]), vec(ln3_b[l]), seq)
        h = _ffn_ln(h, _ffn_w13(ffn2_w13[l]), ffn2_w2[l].astype(BF16), vec(ln4_g[l]), vec(ln4_b[l]))
    return h.reshape(bsz, seq, D_MODEL)
```
